```python
import math
import jax, jax.numpy as jnp
from jax import lax
import numpy as np

D_MODEL = 1024
BATCH = 8
SEQ = 2048
DEPTH = 4

N_BRANCH = 4
MIX_WIDTH = D_MODEL // N_BRANCH
CONV_WIDTH = 3
SSM_GROUP = 16
SSM_GROUPS = MIX_WIDTH // SSM_GROUP
SSM_STATE = 64
DT_MIN = 1e-3
DT_MAX = 1e-1
LAMBDA_RE_MAX = -1e-4
POOL_WINDOWS = (2, 4, 8, 16)
POOL_GROUP = MIX_WIDTH // len(POOL_WINDOWS)
SB_HEAD_DIM = 64
SB_HEADS = MIX_WIDTH // SB_HEAD_DIM
Q_BLOCK = 128
D_FF = 2816
FFN_RES_WEIGHT = 0.5
N_SUB = 3
EPS = 1e-6

CONV_COLS = 3 * MIX_WIDTH
SSM_COLS = MIX_WIDTH
POOL_COLS = MIX_WIDTH
SB_COLS = 3 * MIX_WIDTH
GATE_COLS = N_BRANCH * D_MODEL
IN_COLS = CONV_COLS + SSM_COLS + POOL_COLS + SB_COLS + GATE_COLS
IN_SPLITS = (CONV_COLS, CONV_COLS + SSM_COLS, CONV_COLS + SSM_COLS + POOL_COLS,
             CONV_COLS + SSM_COLS + POOL_COLS + SB_COLS)

kernel_name = "hybrid_parallel_gated_mixer_trunk"


def rmsnorm(x, g):
    xf = x.astype(jnp.float32)
    y = xf * lax.rsqrt(jnp.mean(xf * xf, axis=-1, keepdims=True) + EPS)
    return (y * g.astype(jnp.float32)).astype(x.dtype)


def modulate(h, shift, scale):
    return h * (1.0 + scale[:, None, :]) + shift[:, None, :]


def swiglu_ffn(h, w_in, w_out):
    a, b = jnp.split(h @ w_in, 2, axis=-1)
    return (jax.nn.silu(a) * b) @ w_out


def short_conv_mixer(p, conv_w, w_out):
    b_g, c_g, v = jnp.split(p, 3, axis=-1)
    u = c_g * v
    L = u.shape[1]
    up = jnp.pad(u, ((0, 0), (CONV_WIDTH - 1, 0), (0, 0)))
    y = conv_w[0] * up[:, 0:L]
    for k in range(1, CONV_WIDTH):
        y = y + conv_w[k] * up[:, k:k + L]
    return (b_g * y) @ w_out


def s5_mixer(u, lam_re, lam_im, log_dt, b_re, b_im, c_re, c_im, d_skip, w_glu):
    f32 = jnp.float32
    Bsz, L, W = u.shape
    uf = u.astype(f32).reshape(Bsz, L, SSM_GROUPS, SSM_GROUP)
    lr = jnp.minimum(lam_re.astype(f32), LAMBDA_RE_MAX)
    li = lam_im.astype(f32)
    dt = jnp.exp(log_dt.astype(f32))[:, None]
    mag = jnp.exp(lr * dt)
    ab_re = mag * jnp.cos(li * dt)
    ab_im = mag * jnp.sin(li * dt)
    den = lr * lr + li * li
    nr = ab_re - 1.0
    f_re = (nr * lr + ab_im * li) / den
    f_im = (ab_im * lr - nr * li) / den
    br = b_re.astype(f32)
    bi = b_im.astype(f32)
    bb_re = f_re[..., None] * br - f_im[..., None] * bi
    bb_im = f_re[..., None] * bi + f_im[..., None] * br
    bu_re = jnp.einsum('blgh,gph->blgp', uf, bb_re)
    bu_im = jnp.einsum('blgh,gph->blgp', uf, bb_im)
    a_re = jnp.broadcast_to(ab_re, bu_re.shape)
    a_im = jnp.broadcast_to(ab_im, bu_im.shape)

    def combine(e1, e2):
        a1r, a1i, b1r, b1i = e1
        a2r, a2i, b2r, b2i = e2
        return (a2r * a1r - a2i * a1i,
                a2r * a1i + a2i * a1r,
                a2r * b1r - a2i * b1i + b2r,
                a2r * b1i + a2i * b1r + b2i)

    _, _, s_re, s_im = lax.associative_scan(combine, (a_re, a_im, bu_re, bu_im), axis=1)
    y = (jnp.einsum('blgp,ghp->blgh', s_re, c_re.astype(f32))
         - jnp.einsum('blgp,ghp->blgh', s_im, c_im.astype(f32)))
    y = y.reshape(Bsz, L, W) + d_skip.astype(f32) * uf.reshape(Bsz, L, W)
    y = jax.nn.gelu(y).astype(u.dtype)
    a, g = jnp.split(y @ w_glu, 2, axis=-1)
    return a * jax.nn.sigmoid(g)


def pool_mixer(u, w_pool, pool_scale, w_out):
    f32 = jnp.float32
    Bsz, L, W = u.shape
    uf = u.astype(f32).reshape(Bsz, L, len(POOL_WINDOWS), POOL_GROUP)
    cs = jnp.cumsum(uf, axis=1)
    pos = jnp.arange(L)
    outs = []
    for gi, w in enumerate(POOL_WINDOWS):
        cg = cs[:, :, gi]
        lag = jnp.pad(cg, ((0, 0), (w, 0), (0, 0)))[:, :L]
        cnt = jnp.minimum(pos + 1, w).astype(f32)[None, :, None]
        outs.append((cg - lag) / cnt - uf[:, :, gi])
    pooled = jnp.stack(outs, axis=2)
    mixed = jnp.einsum('blgc,gcd->blgd', pooled, w_pool.astype(f32)).reshape(Bsz, L, W)
    return (mixed * pool_scale.astype(f32)).astype(u.dtype) @ w_out


def stick_breaking_attention(p, w_out):
    f32 = jnp.float32
    Bsz, L, _ = p.shape
    q, k, v = jnp.split(p, 3, axis=-1)
    q = q.astype(f32).reshape(Bsz, L, SB_HEADS, SB_HEAD_DIM) * (SB_HEAD_DIM ** -0.5)
    k = k.astype(f32).reshape(Bsz, L, SB_HEADS, SB_HEAD_DIM)
    v = v.astype(f32).reshape(Bsz, L, SB_HEADS, SB_HEAD_DIM)
    outs = []
    for start in range(0, L, Q_BLOCK):
        end = start + Q_BLOCK
        z = jnp.einsum('bqhd,bkhd->bhqk', q[:, start:end], k[:, :end])
        t_idx = jnp.arange(start, end)[:, None]
        s_idx = jnp.arange(end)[None, :]
        mask = s_idx < t_idx
        log_keep = jnp.where(mask, jax.nn.log_sigmoid(-z), 0.0)
        log_w = jax.nn.log_sigmoid(z) + lax.cumsum(log_keep, axis=3, reverse=True) - log_keep
        a = jnp.where(mask, jnp.exp(log_w), 0.0)
        outs.append(jnp.einsum('bhqk,bkhd->bqhd', a, v[:, :end]))
    o = jnp.concatenate(outs, axis=1).reshape(Bsz, L, MIX_WIDTH).astype(p.dtype)
    return o @ w_out


def setup_inputs(seed: int = 0) -> dict:
    key = jax.random.key(seed)
    ks = jax.random.split(key, 32)
    f32 = jnp.float32

    def nrm(k, shape, fan_in, gain=1.0):
        return jax.random.normal(k, shape, f32) * (gain * fan_in ** -0.5)

    W = MIX_WIDTH
    x = jax.random.normal(ks[0], (BATCH, SEQ, D_MODEL), f32)
    c = jax.random.normal(ks[1], (BATCH, D_MODEL), f32)
    w_ada = nrm(ks[2], (DEPTH, D_MODEL, N_SUB * 3 * D_MODEL), D_MODEL, 0.1)
    b_ada = 0.01 * jax.random.normal(ks[3], (DEPTH, N_SUB * 3 * D_MODEL), f32)
    g_pre = 1.0 + 0.02 * jax.random.normal(ks[4], (DEPTH, N_SUB, D_MODEL), f32)
    g_post = 1.0 + 0.02 * jax.random.normal(ks[5], (DEPTH, N_SUB, D_MODEL), f32)
    w_ff_in = nrm(ks[6], (DEPTH, 2, D_MODEL, 2 * D_FF), D_MODEL)
    w_ff_out = nrm(ks[7], (DEPTH, 2, D_FF, D_MODEL), D_FF)
    w_in = nrm(ks[8], (DEPTH, D_MODEL, IN_COLS), D_MODEL)
    conv_w = nrm(ks[9], (DEPTH, CONV_WIDTH, W), CONV_WIDTH)
    w_conv_out = nrm(ks[10], (DEPTH, W, D_MODEL), W)
    lam_re = -0.5 + 0.01 * jax.random.normal(ks[11], (DEPTH, SSM_GROUPS, SSM_STATE), f32)
    lam_im = (math.pi * jnp.arange(SSM_STATE, dtype=f32))[None, None, :] \
        + 0.01 * jax.random.normal(ks[12], (DEPTH, SSM_GROUPS, SSM_STATE), f32)
    log_dt = jax.random.uniform(ks[13], (DEPTH, SSM_GROUPS), f32,
                                math.log(DT_MIN), math.log(DT_MAX))
    ssm_b_re = nrm(ks[14], (DEPTH, SSM_GROUPS, SSM_STATE, SSM_GROUP), 2 * SSM_GROUP)
    ssm_b_im = nrm(ks[15], (DEPTH, SSM_GROUPS, SSM_STATE, SSM_GROUP), 2 * SSM_GROUP)
    ssm_c_re = nrm(ks[16], (DEPTH, SSM_GROUPS, SSM_GROUP, SSM_STATE), 2 * SSM_STATE)
    ssm_c_im = nrm(ks[17], (DEPTH, SSM_GROUPS, SSM_GROUP, SSM_STATE), 2 * SSM_STATE)
    ssm_d = jax.random.normal(ks[18], (DEPTH, W), f32)
    w_glu = nrm(ks[19], (DEPTH, W, 2 * D_MODEL), W)
    w_pool = nrm(ks[20], (DEPTH, len(POOL_WINDOWS), POOL_GROUP, POOL_GROUP), POOL_GROUP)
    pool_scale = 1.0 + 0.1 * jax.random.normal(ks[21], (DEPTH, W), f32)
    w_pool_out = nrm(ks[22], (DEPTH, W, D_MODEL), W)
    w_sb_out = nrm(ks[23], (DEPTH, W, D_MODEL), W)
    w_out = nrm(ks[24], (DEPTH, D_MODEL, D_MODEL), D_MODEL)
    return {"x": x, "c": c, "w_ada": w_ada, "b_ada": b_ada, "g_pre": g_pre, "g_post": g_post,
            "w_ff_in": w_ff_in, "w_ff_out": w_ff_out, "w_in": w_in, "conv_w": conv_w,
            "w_conv_out": w_conv_out, "lam_re": lam_re, "lam_im": lam_im, "log_dt": log_dt,
            "ssm_b_re": ssm_b_re, "ssm_b_im": ssm_b_im, "ssm_c_re": ssm_c_re, "ssm_c_im": ssm_c_im,
            "ssm_d": ssm_d, "w_glu": w_glu, "w_pool": w_pool, "pool_scale": pool_scale,
            "w_pool_out": w_pool_out, "w_sb_out": w_sb_out, "w_out": w_out}


def reference(x, c, w_ada, b_ada, g_pre, g_post, w_ff_in, w_ff_out, w_in, conv_w,
              w_conv_out, lam_re, lam_im, log_dt, ssm_b_re, ssm_b_im, ssm_c_re, ssm_c_im,
              ssm_d, w_glu, w_pool, pool_scale, w_pool_out, w_sb_out, w_out):
    Bsz, L, D = x.shape
    c_act = jax.nn.silu(c)
    for l in range(DEPTH):
        ada = (c_act @ w_ada[l] + b_ada[l]).reshape(Bsz, N_SUB, 3, D)

        h = modulate(rmsnorm(x, g_pre[l, 0]), ada[:, 0, 0], ada[:, 0, 1])
        f = swiglu_ffn(h, w_ff_in[l, 0], w_ff_out[l, 0])
        x = x + FFN_RES_WEIGHT * (1.0 + ada[:, 0, 2])[:, None, :] * rmsnorm(f, g_post[l, 0])

        h = modulate(rmsnorm(x, g_pre[l, 1]), ada[:, 1, 0], ada[:, 1, 1])
        p = h @ w_in[l]
        p_conv, p_ssm, p_pool, p_sb, p_gate = jnp.split(p, IN_SPLITS, axis=-1)
        y_a = short_conv_mixer(p_conv, conv_w[l], w_conv_out[l])
        y_b = s5_mixer(p_ssm, lam_re[l], lam_im[l], log_dt[l], ssm_b_re[l], ssm_b_im[l],
                       ssm_c_re[l], ssm_c_im[l], ssm_d[l], w_glu[l])
        y_c = pool_mixer(p_pool, w_pool[l], pool_scale[l], w_pool_out[l])
        y_d = stick_breaking_attention(p_sb, w_sb_out[l])
        gates = jax.nn.sigmoid(p_gate).reshape(Bsz, L, N_BRANCH, D)
        merged = (gates[:, :, 0] * y_a + gates[:, :, 1] * y_b
                  + gates[:, :, 2] * y_c + gates[:, :, 3] * y_d)
        m = merged @ w_out[l]
        x = x + (1.0 + ada[:, 1, 2])[:, None, :] * rmsnorm(m, g_post[l, 1])

        h = modulate(rmsnorm(x, g_pre[l, 2]), ada[:, 2, 0], ada[:, 2, 1])
        f = swiglu_ffn(h, w_ff_in[l, 1], w_ff_out[l, 1])
        x = x + FFN_RES_WEIGHT * (1.0 + ada[:, 2, 2])[:, None, :] * rmsnorm(f, g_post[l, 2])
    return x
```

```python
import functools
import math

import jax
import jax.numpy as jnp
from jax import lax
from jax.experimental import pallas as pl
from jax.experimental.pallas import tpu as pltpu

F32 = jnp.float32
BF16 = jnp.bfloat16

D_MODEL = 1024
BATCH = 8
SEQ = 2048
DEPTH = 4
TOKENS = BATCH * SEQ
MIX_WIDTH = 256
CONV_WIDTH = 3
SSM_GROUP = 16
SSM_GROUPS = 16
SSM_STATE = 64
N_STATE = SSM_GROUPS * SSM_STATE
LAMBDA_RE_MAX = -1e-4
POOL_WINDOWS = (2, 4, 8, 16)
POOL_GROUP = 64
SB_HEAD_DIM = 64
SB_HEADS = 4
D_FF = 2816
FFN_RES_WEIGHT = 0.5
N_SUB = 3
EPS = 1e-6
MIX_COLS = 2048

VMEM_LIMIT_BYTES = 56 * 1024 * 1024

ROW_TILE = 512
TILES_PER_SEQ = SEQ // ROW_TILE
FF_CHUNK = 256
SSM_CHUNK = 128
SSM_ROWS = BATCH * SSM_CHUNK
SCAN_LANES = 512
ATT_BLOCK = 128
ADA_COL_TILE = 1536
CONV_HALO = 8
POOL_HALO = 16


def _resident(block_shape, index_map):
    return pl.BlockSpec(block_shape, index_map, pipeline_mode=pl.Buffered(1))


def _params(*semantics):
    return pltpu.CompilerParams(dimension_semantics=semantics,
                                vmem_limit_bytes=VMEM_LIMIT_BYTES)


def _rms(x, g):
    ms = jnp.mean(x * x, axis=-1, keepdims=True)
    return x * lax.rsqrt(ms + EPS) * g


def _sigmoid(x):
    return 1.0 / (1.0 + jnp.exp(-x))


def _dot(a, b):
    return jnp.dot(a, b, preferred_element_type=F32)


def _ada_kernel(c_ref, w_ref, b_ref, o_ref):
    c = c_ref[...]
    c_act = (c * _sigmoid(c)).astype(BF16)
    o_ref[...] = _dot(c_act, w_ref[...].astype(BF16)) + b_ref[...]


def _ada_all(c, w_ada, b_ada):
    n_cols = N_SUB * 3 * D_MODEL
    out = pl.pallas_call(
        _ada_kernel,
        grid=(DEPTH, n_cols // ADA_COL_TILE),
        in_specs=[
            pl.BlockSpec((BATCH, D_MODEL), lambda l, j: (0, 0)),
            pl.BlockSpec((None, D_MODEL, ADA_COL_TILE), lambda l, j: (l, 0, j)),
            pl.BlockSpec((None, 1, ADA_COL_TILE), lambda l, j: (l, 0, j)),
        ],
        out_specs=pl.BlockSpec((None, BATCH, ADA_COL_TILE), lambda l, j: (l, 0, j)),
        out_shape=jax.ShapeDtypeStruct((DEPTH, BATCH, n_cols), F32),
        compiler_params=_params("parallel", "parallel"),
        name="ada",
    )(c, w_ada, b_ada.reshape(DEPTH, 1, n_cols))
    return out.reshape(DEPTH, BATCH, N_SUB * 3, D_MODEL)


def _ffn_kernel(sub, x_ref, ada_ref, gpre_ref, gpost_ref, win_ref, wout_ref, o_ref, g_s):
    x = x_ref[...]
    shift = ada_ref[3 * sub:3 * sub + 1, :]
    scale = ada_ref[3 * sub + 1:3 * sub + 2, :]
    gate = ada_ref[3 * sub + 2:3 * sub + 3, :]
    h = (_rms(x, gpre_ref[sub:sub + 1, :]) * (1.0 + scale) + shift).astype(BF16)
    for j in range(D_FF // FF_CHUNK):
        lo = j * FF_CHUNK
        a = _dot(h, win_ref[:, lo:lo + FF_CHUNK])
        b = _dot(h, win_ref[:, D_FF + lo:D_FF + lo + FF_CHUNK])
        g_s[:, lo:lo + FF_CHUNK] = (a * _sigmoid(a) * b).astype(BF16)
    f = _dot(g_s[...], wout_ref[...])
    o_ref[...] = x + FFN_RES_WEIGHT * (1.0 + gate) * _rms(f, gpost_ref[sub:sub + 1, :])


def _ffn(x, ada, g_pre, g_post, w_in, w_out, layer, which):
    sub = 0 if which == 0 else 2
    return pl.pallas_call(
        functools.partial(_ffn_kernel, sub),
        grid=(TOKENS // ROW_TILE,),
        in_specs=[
            pl.BlockSpec((ROW_TILE, D_MODEL), lambda i: (i, 0)),
            pl.BlockSpec((None, None, N_SUB * 3, D_MODEL),
                         lambda i: (layer, i // TILES_PER_SEQ, 0, 0)),
            _resident((None, N_SUB, D_MODEL), lambda i: (layer, 0, 0)),
            _resident((None, N_SUB, D_MODEL), lambda i: (layer, 0, 0)),
            _resident((None, None, D_MODEL, 2 * D_FF), lambda i: (layer, which, 0, 0)),
            _resident((None, None, D_FF, D_MODEL), lambda i: (layer, which, 0, 0)),
        ],
        out_specs=pl.BlockSpec((ROW_TILE, D_MODEL), lambda i: (i, 0)),
        out_shape=jax.ShapeDtypeStruct((TOKENS, D_MODEL), F32),
        scratch_shapes=[pltpu.VMEM((ROW_TILE, D_FF), BF16)],
        compiler_params=_params("parallel"),
        name=f"ffn{which}",
    )(x, ada, g_pre, g_post, w_in, w_out)


def _inproj_kernel(x_ref, ada_ref, gpre_ref, w_ref, oc_ref, os_ref, op_ref, ob_ref):
    x = x_ref[...]
    shift = ada_ref[3:4, :]
    scale = ada_ref[4:5, :]
    h = (_rms(x, gpre_ref[1:2, :]) * (1.0 + scale) + shift).astype(BF16)
    p = _dot(h, w_ref[...])
    oc_ref[...] = p[:, 0:768]
    os_ref[...] = p[:, 768:1024]
    op_ref[...] = p[:, 1024:1280]
    ob_ref[...] = p[:, 1280:2048]


def _inproj(x, ada, g_pre, w_mix, layer):
    widths = (768, 256, 256, 768)
    return pl.pallas_call(
        _inproj_kernel,
        grid=(TOKENS // ROW_TILE,),
        in_specs=[
            pl.BlockSpec((ROW_TILE, D_MODEL), lambda i: (i, 0)),
            pl.BlockSpec((None, None, N_SUB * 3, D_MODEL),
                         lambda i: (layer, i // TILES_PER_SEQ, 0, 0)),
            _resident((None, N_SUB, D_MODEL), lambda i: (layer, 0, 0)),
            _resident((None, D_MODEL, MIX_COLS), lambda i: (layer, 0, 0)),
        ],
        out_specs=[pl.BlockSpec((ROW_TILE, w), lambda i: (i, 0)) for w in widths],
        out_shape=[jax.ShapeDtypeStruct((TOKENS, w), F32) for w in widths],
        compiler_params=_params("parallel"),
        name="inproj",
    )(x, ada, g_pre, w_mix)


def _ssm_kernel(u_ref, perm_ref, permt_ref, bmat_ref, cmat_ref, a_ref, d_ref, o_ref,
                bu_s, y_s, st_s):
    @pl.when(pl.program_id(0) == 0)
    def _():
        st_s[...] = jnp.zeros_like(st_s)

    u = u_ref[...].reshape(SSM_ROWS, MIX_WIDTH)
    u_tm = _dot(perm_ref[...], u.astype(BF16)).astype(BF16)
    bu_s[...] = _dot(u_tm, bmat_ref[...])

    for c in range(N_STATE // SCAN_LANES):
        re = slice(c * SCAN_LANES, (c + 1) * SCAN_LANES)
        im = slice(N_STATE + c * SCAN_LANES, N_STATE + (c + 1) * SCAN_LANES)
        a_re = a_ref[:, re]
        a_im = a_ref[:, im]

        def step(t, carry, re=re, im=im, a_re=a_re, a_im=a_im):
            s_re, s_im = carry
            rows = pl.ds(pl.multiple_of(t * BATCH, BATCH), BATCH)
            n_re = a_re * s_re - a_im * s_im + bu_s[rows, re]
            n_im = a_re * s_im + a_im * s_re + bu_s[rows, im]
            bu_s[rows, re] = n_re
            bu_s[rows, im] = n_im
            return n_re, n_im

        s_re, s_im = lax.fori_loop(0, SSM_CHUNK, step, (st_s[:, re], st_s[:, im]), unroll=4)
        st_s[:, re] = s_re
        st_s[:, im] = s_im

    for r in range(0, SSM_ROWS, 256):
        y_s[r:r + 256, :] = _dot(bu_s[r:r + 256, :].astype(BF16), cmat_ref[...])
    y = y_s[...]
    y_hi = y.astype(BF16)
    y_lo = (y - y_hi.astype(F32)).astype(BF16)
    y_bm = _dot(permt_ref[...], y_hi) + _dot(permt_ref[...], y_lo)
    y_bm = y_bm + d_ref[...] * u
    o_ref[...] = jax.nn.gelu(y_bm).reshape(BATCH, SSM_CHUNK, MIX_WIDTH)


def _ssm(p_ssm, perm, perm_t, bmat, cmat, a_bcast, d_skip, layer):
    return pl.pallas_call(
        _ssm_kernel,
        grid=(SEQ // SSM_CHUNK,),
        in_specs=[
            pl.BlockSpec((BATCH, SSM_CHUNK, MIX_WIDTH), lambda i: (0, i, 0)),
            _resident((SSM_ROWS, SSM_ROWS), lambda i: (0, 0)),
            _resident((SSM_ROWS, SSM_ROWS), lambda i: (0, 0)),
            _resident((None, MIX_WIDTH, 2 * N_STATE), lambda i: (layer, 0, 0)),
            _resident((None, 2 * N_STATE, MIX_WIDTH), lambda i: (layer, 0, 0)),
            _resident((None, BATCH, 2 * N_STATE), lambda i: (layer, 0, 0)),
            _resident((None, 1, MIX_WIDTH), lambda i: (layer, 0, 0)),
        ],
        out_specs=pl.BlockSpec((BATCH, SSM_CHUNK, MIX_WIDTH), lambda i: (0, i, 0)),
        out_shape=jax.ShapeDtypeStruct((BATCH, SEQ, MIX_WIDTH), F32),
        scratch_shapes=[
            pltpu.VMEM((SSM_ROWS, 2 * N_STATE), F32),
            pltpu.VMEM((SSM_ROWS, MIX_WIDTH), F32),
            pltpu.VMEM((BATCH, 2 * N_STATE), F32),
        ],
        compiler_params=_params("arbitrary"),
        name="ssm",
    )(p_ssm, perm, perm_t, bmat, cmat, a_bcast, d_skip)


def _ssm_matrices(lam_re, lam_im, log_dt, b_re, b_im, c_re, c_im):
    lr = jnp.minimum(lam_re, LAMBDA_RE_MAX)
    li = lam_im
    dt = jnp.exp(log_dt)[:, None]
    mag = jnp.exp(lr * dt)
    ab_re = mag * jnp.cos(li * dt)
    ab_im = mag * jnp.sin(li * dt)
    den = lr * lr + li * li
    nr = ab_re - 1.0
    f_re = (nr * lr + ab_im * li) / den
    f_im = (ab_im * lr - nr * li) / den
    bb_re = f_re[..., None] * b_re - f_im[..., None] * b_im
    bb_im = f_re[..., None] * b_im + f_im[..., None] * b_re
    eye = jnp.eye(SSM_GROUPS, dtype=F32)
    to_b = lambda m: jnp.einsum('gph,gk->ghkp', m, eye).reshape(MIX_WIDTH, N_STATE)
    to_c = lambda m: jnp.einsum('ghp,gk->gpkh', m, eye).reshape(N_STATE, MIX_WIDTH)
    bmat = jnp.concatenate([to_b(bb_re), to_b(bb_im)], axis=1).astype(BF16)
    cmat = jnp.concatenate([to_c(c_re), -to_c(c_im)], axis=0).astype(BF16)
    a = jnp.concatenate([ab_re.reshape(N_STATE), ab_im.reshape(N_STATE)])
    return bmat, cmat, jnp.broadcast_to(a, (BATCH, 2 * N_STATE))


def _attn_block(q, k, v, tri, r_sum, acc, diagonal):
    z = lax.dot_general(q, k, (((1,), (1,)), ((), ())), preferred_element_type=F32)
    log_keep = -(jnp.maximum(z, 0.0) + jnp.log1p(jnp.exp(-jnp.abs(z))))
    if diagonal:
        t_idx = lax.broadcasted_iota(jnp.int32, z.shape, 0)
        s_idx = lax.broadcasted_iota(jnp.int32, z.shape, 1)
        mask = s_idx < t_idx
        log_keep = jnp.where(mask, log_keep, 0.0)
    hi = log_keep.astype(BF16)
    lo = (log_keep - hi.astype(F32)).astype(BF16)
    suffix = _dot(hi, tri) + _dot(lo, tri)
    log_w = z + suffix + r_sum
    w = jnp.exp(log_w)
    if diagonal:
        w = jnp.where(mask, w, 0.0)
    acc = acc + _dot(w.astype(BF16), v)
    return r_sum + suffix[:, 0:1], acc


def _attn_kernel(p_ref, o_ref, q_s, k_s, v_s):
    row = lax.broadcasted_iota(jnp.int32, (ATT_BLOCK, ATT_BLOCK), 0)
    col = lax.broadcasted_iota(jnp.int32, (ATT_BLOCK, ATT_BLOCK), 1)
    tri = (row >= col).astype(BF16)
    for h in range(SB_HEADS):
        lanes = slice(h * SB_HEAD_DIM, (h + 1) * SB_HEAD_DIM)
        q_s[...] = (p_ref[:, h * SB_HEAD_DIM:(h + 1) * SB_HEAD_DIM]
                    * (SB_HEAD_DIM ** -0.5)).astype(BF16)
        k_s[...] = p_ref[:, MIX_WIDTH + h * SB_HEAD_DIM:MIX_WIDTH + (h + 1) * SB_HEAD_DIM].astype(BF16)
        v_s[...] = p_ref[:, 2 * MIX_WIDTH + h * SB_HEAD_DIM:
                         2 * MIX_WIDTH + (h + 1) * SB_HEAD_DIM].astype(BF16)

        def q_block(i, _, lanes=lanes):
            q_rows = pl.ds(pl.multiple_of(i * ATT_BLOCK, ATT_BLOCK), ATT_BLOCK)
            q = q_s[q_rows, :]
            r0 = jnp.zeros((ATT_BLOCK, 1), F32)
            acc0 = jnp.zeros((ATT_BLOCK, SB_HEAD_DIM), F32)
            r_sum, acc = _attn_block(q, k_s[q_rows, :], v_s[q_rows, :], tri, r0, acc0, True)

            def k_block(j, carry):
                k_rows = pl.ds(pl.multiple_of((i - j) * ATT_BLOCK, ATT_BLOCK), ATT_BLOCK)
                return _attn_block(q, k_s[k_rows, :], v_s[k_rows, :], tri, *carry, False)

            r_sum, acc = lax.fori_loop(1, i + 1, k_block, (r_sum, acc))
            o_ref[q_rows, lanes] = acc
            return 0

        lax.fori_loop(0, SEQ // ATT_BLOCK, q_block, 0)


def _attn(p_sb):
    return pl.pallas_call(
        _attn_kernel,
        grid=(BATCH,),
        in_specs=[pl.BlockSpec((None, SEQ, 3 * MIX_WIDTH), lambda b: (b, 0, 0))],
        out_specs=pl.BlockSpec((None, SEQ, MIX_WIDTH), lambda b: (b, 0, 0)),
        out_shape=jax.ShapeDtypeStruct((BATCH, SEQ, MIX_WIDTH), F32),
        scratch_shapes=[pltpu.VMEM((SEQ, SB_HEAD_DIM), BF16)] * 3,
        compiler_params=_params("parallel"),
        name="attn",
    )(p_sb)


def _merge_kernel(x_ref, ada_ref, gpre_ref, gpost_ref, pc_ref, pch_ref, pp_ref, pph_ref,
                  ys_ref, oa_ref, convw_ref, wg_ref, wco_ref, wglu_ref, wpool_ref, pscale_ref,
                  wpo_ref, wso_ref, wout_ref, o_ref):
    t_tile = pl.program_id(0) % TILES_PER_SEQ
    not_first = (t_tile != 0).astype(F32)
    x = x_ref[...]
    shift = ada_ref[3:4, :]
    scale = ada_ref[4:5, :]
    gate = ada_ref[5:6, :]
    h = (_rms(x, gpre_ref[1:2, :]) * (1.0 + scale) + shift).astype(BF16)

    def gated(branch, y):
        g = _sigmoid(_dot(h, wg_ref[:, branch * D_MODEL:(branch + 1) * D_MODEL]))
        return g * y

    pc = pc_ref[...]
    pch = pch_ref[...]
    u = pc[:, 256:512] * pc[:, 512:768]
    u_halo = pch[:, 256:512] * pch[:, 512:768] * not_first
    u_ext = jnp.concatenate([u_halo, u], axis=0)
    u_1 = pltpu.roll(u_ext, 1, 0)[CONV_HALO:, :]
    u_2 = pltpu.roll(u_ext, 2, 0)[CONV_HALO:, :]
    conv = convw_ref[0:1, :] * u_2 + convw_ref[1:2, :] * u_1 + convw_ref[2:3, :] * u
    merged = gated(0, _dot((pc[:, 0:256] * conv).astype(BF16), wco_ref[...]))

    glu = _dot(ys_ref[...].astype(BF16), wglu_ref[...])
    merged = merged + gated(1, glu[:, :D_MODEL] * _sigmoid(glu[:, D_MODEL:]))

    up = pp_ref[...]
    s1 = jnp.concatenate([pph_ref[...] * not_first, up], axis=0)
    s2 = s1 + pltpu.roll(s1, 1, 0)
    s4 = s2 + pltpu.roll(s2, 2, 0)
    s8 = s4 + pltpu.roll(s4, 4, 0)
    s16 = s8 + pltpu.roll(s8, 8, 0)
    pos1 = (t_tile * ROW_TILE + 1
            + lax.broadcasted_iota(jnp.int32, (ROW_TILE, 1), 0)).astype(F32)
    lane = lax.broadcasted_iota(jnp.int32, (ROW_TILE, 128), 1)
    low = lane < POOL_GROUP

    def window_mean(s, w):
        return s[POOL_HALO:, :] / jnp.minimum(pos1, float(w))

    m_a = jnp.where(low, window_mean(s2, 2)[:, :128], window_mean(s4, 4)[:, :128])
    m_b = jnp.where(low, window_mean(s8, 8)[:, 128:], window_mean(s16, 16)[:, 128:])
    pooled = jnp.concatenate([m_a, m_b], axis=1) - up
    mixed = _dot(pooled.astype(BF16), wpool_ref[...]) * pscale_ref[...]
    merged = merged + gated(2, _dot(mixed.astype(BF16), wpo_ref[...]))

    merged = merged + gated(3, _dot(oa_ref[...].astype(BF16), wso_ref[...]))

    m = _dot(merged.astype(BF16), wout_ref[...])
    o_ref[...] = x + (1.0 + gate) * _rms(m, gpost_ref[1:2, :])


def _merge(x, ada, g_pre, g_post, p_conv, p_pool, y_ssm, o_att, conv_w, w_gate, w_conv_out,
           w_glu, w_pool_bd, pool_scale, w_pool_out, w_sb_out, w_out, layer):
    row = lambda w: pl.BlockSpec((ROW_TILE, w), lambda i: (i, 0))
    halo = lambda n, w: pl.BlockSpec(
        (n, w), lambda i: (jnp.maximum(i * (ROW_TILE // n) - 1, 0), 0))
    per_layer = lambda *shape: _resident((None,) + shape, lambda i: (layer,) + (0,) * len(shape))
    return pl.pallas_call(
        _merge_kernel,
        grid=(TOKENS // ROW_TILE,),
        in_specs=[
            row(D_MODEL),
            pl.BlockSpec((None, None, N_SUB * 3, D_MODEL),
                         lambda i: (layer, i // TILES_PER_SEQ, 0, 0)),
            per_layer(N_SUB, D_MODEL),
            per_layer(N_SUB, D_MODEL),
            row(3 * MIX_WIDTH), halo(CONV_HALO, 3 * MIX_WIDTH),
            row(MIX_WIDTH), halo(POOL_HALO, MIX_WIDTH),
            row(MIX_WIDTH), row(MIX_WIDTH),
            per_layer(CONV_WIDTH, MIX_WIDTH),
            per_layer(D_MODEL, 4 * D_MODEL),
            per_layer(MIX_WIDTH, D_MODEL),
            per_layer(MIX_WIDTH, 2 * D_MODEL),
            per_layer(MIX_WIDTH, MIX_WIDTH),
            per_layer(1, MIX_WIDTH),
            per_layer(MIX_WIDTH, D_MODEL),
            per_layer(MIX_WIDTH, D_MODEL),
            per_layer(D_MODEL, D_MODEL),
        ],
        out_specs=row(D_MODEL),
        out_shape=jax.ShapeDtypeStruct((TOKENS, D_MODEL), F32),
        compiler_params=_params("parallel"),
        name="merge",
    )(x, ada, g_pre, g_post, p_conv, p_conv, p_pool, p_pool, y_ssm, o_att, conv_w, w_gate,
      w_conv_out, w_glu, w_pool_bd, pool_scale, w_pool_out, w_sb_out, w_out)


def _time_major_permutation():
    r = jnp.arange(SSM_ROWS)
    src = (r % BATCH) * SSM_CHUNK + r // BATCH
    return (src[:, None] == r[None, :]).astype(BF16)


def kernel(x, c, w_ada, b_ada, g_pre, g_post, w_ff_in, w_ff_out, w_in, conv_w, w_conv_out,
           lam_re, lam_im, log_dt, ssm_b_re, ssm_b_im, ssm_c_re, ssm_c_im, ssm_d, w_glu, w_pool,
           pool_scale, w_pool_out, w_sb_out, w_out):
    ada = _ada_all(c, w_ada, b_ada)

    w_ff_in_b = w_ff_in.astype(BF16)
    w_ff_out_b = w_ff_out.astype(BF16)
    w_mix_b = w_in[:, :, :MIX_COLS].astype(BF16)
    w_gate_b = w_in[:, :, MIX_COLS:].astype(BF16)
    w_conv_out_b = w_conv_out.astype(BF16)
    w_glu_b = w_glu.astype(BF16)
    w_pool_out_b = w_pool_out.astype(BF16)
    w_sb_out_b = w_sb_out.astype(BF16)
    w_out_b = w_out.astype(BF16)
    eye = jnp.eye(len(POOL_WINDOWS), dtype=F32)
    w_pool_bd = jnp.einsum('lgcd,gk->lgckd', w_pool, eye).reshape(
        DEPTH, MIX_WIDTH, MIX_WIDTH).astype(BF16)
    pool_scale_r = pool_scale.reshape(DEPTH, 1, MIX_WIDTH)
    d_skip = ssm_d.reshape(DEPTH, 1, MIX_WIDTH)
    bmat, cmat, a_bcast = jax.vmap(_ssm_matrices)(
        lam_re, lam_im, log_dt, ssm_b_re, ssm_b_im, ssm_c_re, ssm_c_im)
    perm = _time_major_permutation()
    perm_t = perm.T

    xt = x.reshape(TOKENS, D_MODEL)
    for l in range(DEPTH):
        xt = _ffn(xt, ada, g_pre, g_post, w_ff_in_b, w_ff_out_b, l, 0)
        p_conv, p_ssm, p_pool, p_sb = _inproj(xt, ada, g_pre, w_mix_b, l)
        y_ssm = _ssm(p_ssm.reshape(BATCH, SEQ, MIX_WIDTH), perm, perm_t, bmat, cmat, a_bcast,
                     d_skip, l)
        o_att = _attn(p_sb.reshape(BATCH, SEQ, 3 * MIX_WIDTH))
        xt = _merge(xt, ada, g_pre, g_post, p_conv, p_pool,
                    y_ssm.reshape(TOKENS, MIX_WIDTH), o_att.reshape(TOKENS, MIX_WIDTH),
                    conv_w, w_gate_b, w_conv_out_b, w_glu_b, w_pool_bd, pool_scale_r,
                    w_pool_out_b, w_sb_out_b, w_out_b, l)
        xt = _ffn(xt, ada, g_pre, g_post, w_ff_in_b, w_ff_out_b, l, 1)
    return xt.reshape(BATCH, SEQ, D_MODEL)
```

```python
import functools
import math

import jax
import jax.numpy as jnp
from jax import lax
from jax.experimental import pallas as pl
from jax.experimental.pallas import tpu as pltpu

F32 = jnp.float32
BF16 = jnp.bfloat16

D_MODEL = 1024
BATCH = 8
SEQ = 2048
DEPTH = 4
TOKENS = BATCH * SEQ
MIX_WIDTH = 256
CONV_WIDTH = 3
SSM_GROUP = 16
SSM_GROUPS = 16
SSM_STATE = 64
N_STATE = SSM_GROUPS * SSM_STATE
LAMBDA_RE_MAX = -1e-4
POOL_WINDOWS = (2, 4, 8, 16)
POOL_GROUP = 64
SB_HEAD_DIM = 64
SB_HEADS = 4
D_FF = 2816
FFN_RES_WEIGHT = 0.5
N_SUB = 3
EPS = 1e-6
MIX_COLS = 2048

VMEM_LIMIT_BYTES = 56 * 1024 * 1024

ROW_TILE = 512
TILES_PER_SEQ = SEQ // ROW_TILE
FF_CHUNK = 256
SSM_CHUNK = 128
SSM_ROWS = BATCH * SSM_CHUNK
SCAN_LANES = 512
ATT_BLOCK = 256
ADA_COL_TILE = 1536
CONV_HALO = 8
POOL_HALO = 16


def _resident(block_shape, index_map):
    return pl.BlockSpec(block_shape, index_map, pipeline_mode=pl.Buffered(1))


def _params(*semantics):
    return pltpu.CompilerParams(dimension_semantics=semantics,
                                vmem_limit_bytes=VMEM_LIMIT_BYTES)


def _rms(x, g):
    ms = jnp.mean(x * x, axis=-1, keepdims=True)
    return x * lax.rsqrt(ms + EPS) * g


def _sigmoid(x):
    return 1.0 / (1.0 + jnp.exp(-x))


def _dot(a, b):
    return jnp.dot(a, b, preferred_element_type=F32)


def _ada_kernel(c_ref, w_ref, b_ref, o_ref):
    c = c_ref[...]
    c_act = (c * _sigmoid(c)).astype(BF16)
    o_ref[...] = _dot(c_act, w_ref[...].astype(BF16)) + b_ref[...]


def _ada_all(c, w_ada, b_ada):
    n_cols = N_SUB * 3 * D_MODEL
    out = pl.pallas_call(
        _ada_kernel,
        grid=(DEPTH, n_cols // ADA_COL_TILE),
        in_specs=[
            pl.BlockSpec((BATCH, D_MODEL), lambda l, j: (0, 0)),
            pl.BlockSpec((None, D_MODEL, ADA_COL_TILE), lambda l, j: (l, 0, j)),
            pl.BlockSpec((None, 1, ADA_COL_TILE), lambda l, j: (l, 0, j)),
        ],
        out_specs=pl.BlockSpec((None, BATCH, ADA_COL_TILE), lambda l, j: (l, 0, j)),
        out_shape=jax.ShapeDtypeStruct((DEPTH, BATCH, n_cols), F32),
        compiler_params=_params("parallel", "parallel"),
        name="ada",
    )(c, w_ada, b_ada.reshape(DEPTH, 1, n_cols))
    return out.reshape(DEPTH, BATCH, N_SUB * 3, D_MODEL)


def _ffn_kernel(sub, x_ref, ada_ref, gpre_ref, gpost_ref, win_ref, wout_ref, o_ref, g_s):
    x = x_ref[...]
    shift = ada_ref[3 * sub:3 * sub + 1, :]
    scale = ada_ref[3 * sub + 1:3 * sub + 2, :]
    gate = ada_ref[3 * sub + 2:3 * sub + 3, :]
    h = (_rms(x, gpre_ref[sub:sub + 1, :]) * (1.0 + scale) + shift).astype(BF16)
    for j in range(D_FF // FF_CHUNK):
        lo = j * FF_CHUNK
        a = _dot(h, win_ref[:, lo:lo + FF_CHUNK])
        b = _dot(h, win_ref[:, D_FF + lo:D_FF + lo + FF_CHUNK])
        g_s[:, lo:lo + FF_CHUNK] = (a * _sigmoid(a) * b).astype(BF16)
    f = _dot(g_s[...], wout_ref[...])
    o_ref[...] = x + FFN_RES_WEIGHT * (1.0 + gate) * _rms(f, gpost_ref[sub:sub + 1, :])


def _ffn(x, ada, g_pre, g_post, w_in, w_out, layer, which):
    sub = 0 if which == 0 else 2
    return pl.pallas_call(
        functools.partial(_ffn_kernel, sub),
        grid=(TOKENS // ROW_TILE,),
        in_specs=[
            pl.BlockSpec((ROW_TILE, D_MODEL), lambda i: (i, 0)),
            pl.BlockSpec((None, None, N_SUB * 3, D_MODEL),
                         lambda i: (layer, i // TILES_PER_SEQ, 0, 0)),
            _resident((None, N_SUB, D_MODEL), lambda i: (layer, 0, 0)),
            _resident((None, N_SUB, D_MODEL), lambda i: (layer, 0, 0)),
            _resident((None, None, D_MODEL, 2 * D_FF), lambda i: (layer, which, 0, 0)),
            _resident((None, None, D_FF, D_MODEL), lambda i: (layer, which, 0, 0)),
        ],
        out_specs=pl.BlockSpec((ROW_TILE, D_MODEL), lambda i: (i, 0)),
        out_shape=jax.ShapeDtypeStruct((TOKENS, D_MODEL), F32),
        scratch_shapes=[pltpu.VMEM((ROW_TILE, D_FF), BF16)],
        compiler_params=_params("parallel"),
        name=f"ffn{which}",
    )(x, ada, g_pre, g_post, w_in, w_out)


def _inproj_kernel(x_ref, ada_ref, gpre_ref, w_ref, oc_ref, os_ref, op_ref, ob_ref):
    x = x_ref[...]
    shift = ada_ref[3:4, :]
    scale = ada_ref[4:5, :]
    h = (_rms(x, gpre_ref[1:2, :]) * (1.0 + scale) + shift).astype(BF16)
    p = _dot(h, w_ref[...])
    oc_ref[...] = p[:, 0:768]
    os_ref[...] = p[:, 768:1024]
    op_ref[...] = p[:, 1024:1280]
    ob_ref[...] = p[:, 1280:2048]


def _inproj(x, ada, g_pre, w_mix, layer):
    widths = (768, 256, 256, 768)
    return pl.pallas_call(
        _inproj_kernel,
        grid=(TOKENS // ROW_TILE,),
        in_specs=[
            pl.BlockSpec((ROW_TILE, D_MODEL), lambda i: (i, 0)),
            pl.BlockSpec((None, None, N_SUB * 3, D_MODEL),
                         lambda i: (layer, i // TILES_PER_SEQ, 0, 0)),
            _resident((None, N_SUB, D_MODEL), lambda i: (layer, 0, 0)),
            _resident((None, D_MODEL, MIX_COLS), lambda i: (layer, 0, 0)),
        ],
        out_specs=[pl.BlockSpec((ROW_TILE, w), lambda i: (i, 0)) for w in widths],
        out_shape=[jax.ShapeDtypeStruct((TOKENS, w), F32) for w in widths],
        compiler_params=_params("parallel"),
        name="inproj",
    )(x, ada, g_pre, w_mix)


def _ssm_kernel(u_ref, perm_ref, permt_ref, bmat_ref, cmat_ref, a_ref, d_ref, o_ref,
                bu_s, y_s, st_s):
    @pl.when(pl.program_id(0) == 0)
    def _():
        st_s[...] = jnp.zeros_like(st_s)

    u = u_ref[...].reshape(SSM_ROWS, MIX_WIDTH)
    u_tm = _dot(perm_ref[...], u.astype(BF16)).astype(BF16)
    bu_s[...] = _dot(u_tm, bmat_ref[...])

    for c in range(N_STATE // SCAN_LANES):
        re = slice(c * SCAN_LANES, (c + 1) * SCAN_LANES)
        im = slice(N_STATE + c * SCAN_LANES, N_STATE + (c + 1) * SCAN_LANES)
        a_re = a_ref[:, re]
        a_im = a_ref[:, im]

        def step(t, carry, re=re, im=im, a_re=a_re, a_im=a_im):
            s_re, s_im = carry
            rows = pl.ds(pl.multiple_of(t * BATCH, BATCH), BATCH)
            n_re = a_re * s_re - a_im * s_im + bu_s[rows, re]
            n_im = a_re * s_im + a_im * s_re + bu_s[rows, im]
            bu_s[rows, re] = n_re
            bu_s[rows, im] = n_im
            return n_re, n_im

        s_re, s_im = lax.fori_loop(0, SSM_CHUNK, step, (st_s[:, re], st_s[:, im]), unroll=4)
        st_s[:, re] = s_re
        st_s[:, im] = s_im

    for r in range(0, SSM_ROWS, 256):
        y_s[r:r + 256, :] = _dot(bu_s[r:r + 256, :].astype(BF16), cmat_ref[...])
    y = y_s[...]
    y_hi = y.astype(BF16)
    y_lo = (y - y_hi.astype(F32)).astype(BF16)
    y_bm = _dot(permt_ref[...], y_hi) + _dot(permt_ref[...], y_lo)
    y_bm = y_bm + d_ref[...] * u
    o_ref[...] = jax.nn.gelu(y_bm).reshape(BATCH, SSM_CHUNK, MIX_WIDTH)


def _ssm(p_ssm, perm, perm_t, bmat, cmat, a_bcast, d_skip, layer):
    return pl.pallas_call(
        _ssm_kernel,
        grid=(SEQ // SSM_CHUNK,),
        in_specs=[
            pl.BlockSpec((BATCH, SSM_CHUNK, MIX_WIDTH), lambda i: (0, i, 0)),
            _resident((SSM_ROWS, SSM_ROWS), lambda i: (0, 0)),
            _resident((SSM_ROWS, SSM_ROWS), lambda i: (0, 0)),
            _resident((None, MIX_WIDTH, 2 * N_STATE), lambda i: (layer, 0, 0)),
            _resident((None, 2 * N_STATE, MIX_WIDTH), lambda i: (layer, 0, 0)),
            _resident((None, BATCH, 2 * N_STATE), lambda i: (layer, 0, 0)),
            _resident((None, 1, MIX_WIDTH), lambda i: (layer, 0, 0)),
        ],
        out_specs=pl.BlockSpec((BATCH, SSM_CHUNK, MIX_WIDTH), lambda i: (0, i, 0)),
        out_shape=jax.ShapeDtypeStruct((BATCH, SEQ, MIX_WIDTH), F32),
        scratch_shapes=[
            pltpu.VMEM((SSM_ROWS, 2 * N_STATE), F32),
            pltpu.VMEM((SSM_ROWS, MIX_WIDTH), F32),
            pltpu.VMEM((BATCH, 2 * N_STATE), F32),
        ],
        compiler_params=_params("arbitrary"),
        name="ssm",
    )(p_ssm, perm, perm_t, bmat, cmat, a_bcast, d_skip)


def _ssm_matrices(lam_re, lam_im, log_dt, b_re, b_im, c_re, c_im):
    lr = jnp.minimum(lam_re, LAMBDA_RE_MAX)
    li = lam_im
    dt = jnp.exp(log_dt)[:, None]
    mag = jnp.exp(lr * dt)
    ab_re = mag * jnp.cos(li * dt)
    ab_im = mag * jnp.sin(li * dt)
    den = lr * lr + li * li
    nr = ab_re - 1.0
    f_re = (nr * lr + ab_im * li) / den
    f_im = (ab_im * lr - nr * li) / den
    bb_re = f_re[..., None] * b_re - f_im[..., None] * b_im
    bb_im = f_re[..., None] * b_im + f_im[..., None] * b_re
    eye = jnp.eye(SSM_GROUPS, dtype=F32)
    to_b = lambda m: jnp.einsum('gph,gk->ghkp', m, eye).reshape(MIX_WIDTH, N_STATE)
    to_c = lambda m: jnp.einsum('ghp,gk->gpkh', m, eye).reshape(N_STATE, MIX_WIDTH)
    bmat = jnp.concatenate([to_b(bb_re), to_b(bb_im)], axis=1).astype(BF16)
    cmat = jnp.concatenate([to_c(c_re), -to_c(c_im)], axis=0).astype(BF16)
    a = jnp.concatenate([ab_re.reshape(N_STATE), ab_im.reshape(N_STATE)])
    return bmat, cmat, jnp.broadcast_to(a, (BATCH, 2 * N_STATE))


def _attn_step(q_rows, k_rows, diagonal, q_s, k_s, v_s, acc_s, r_s, neg_tri, mask):
    heads = range(SB_HEADS)
    zs, hls, ws = [], [], []
    for h in heads:
        z = lax.dot_general(q_s[h, q_rows, :], k_s[h, k_rows, :], (((1,), (1,)), ((), ())),
                            preferred_element_type=F32)
        drop = jnp.maximum(z, 0.0) + jnp.log(1.0 + jnp.exp(-jnp.abs(z)))
        if diagonal:
            drop = jnp.where(mask, drop, 0.0)
        hi = drop.astype(BF16)
        lo = (drop - hi.astype(F32)).astype(BF16)
        zs.append(z)
        hls.append(jnp.concatenate([hi, lo], axis=1))
    for h in heads:
        suffix = _dot(hls[h], neg_tri)
        total = jnp.broadcast_to(suffix[:, 0:1], (ATT_BLOCK, 128))
        log_w = zs[h] + suffix
        if diagonal:
            r_s[h] = total
        else:
            r = r_s[h]
            log_w = log_w + jnp.concatenate([r] * (ATT_BLOCK // 128), axis=1)
            r_s[h] = r + total
        w = jnp.exp(log_w)
        if diagonal:
            w = jnp.where(mask, w, 0.0)
        ws.append(w.astype(BF16))
    for h in heads:
        pv = _dot(ws[h], v_s[h, k_rows, :])
        if diagonal:
            acc_s[h] = pv
        else:
            acc_s[h] += pv


def _attn_kernel(p_ref, o_ref, q_s, k_s, v_s, acc_s, r_s):
    row = lax.broadcasted_iota(jnp.int32, (ATT_BLOCK, ATT_BLOCK), 0)
    col = lax.broadcasted_iota(jnp.int32, (ATT_BLOCK, ATT_BLOCK), 1)
    mask = col < row
    neg_tri = jnp.where(row >= col, -1.0, 0.0).astype(BF16)
    neg_tri = jnp.concatenate([neg_tri, neg_tri], axis=0)
    for h in range(SB_HEADS):
        lanes = slice(h * SB_HEAD_DIM, (h + 1) * SB_HEAD_DIM)
        q_s[h] = (p_ref[:, lanes] * (SB_HEAD_DIM ** -0.5)).astype(BF16)
        k_s[h] = p_ref[:, MIX_WIDTH + lanes.start:MIX_WIDTH + lanes.stop].astype(BF16)
        v_s[h] = p_ref[:, 2 * MIX_WIDTH + lanes.start:2 * MIX_WIDTH + lanes.stop].astype(BF16)

    def q_block(i, _):
        q_rows = pl.ds(pl.multiple_of(i * ATT_BLOCK, ATT_BLOCK), ATT_BLOCK)
        step = functools.partial(_attn_step, q_s=q_s, k_s=k_s, v_s=v_s, acc_s=acc_s, r_s=r_s,
                                 neg_tri=neg_tri, mask=mask)
        step(q_rows, q_rows, True)

        def k_block(j, _):
            step(q_rows, pl.ds(pl.multiple_of((i - j) * ATT_BLOCK, ATT_BLOCK), ATT_BLOCK), False)
            return 0

        lax.fori_loop(1, i + 1, k_block, 0)
        o_ref[q_rows, :] = jnp.concatenate([acc_s[h] for h in range(SB_HEADS)], axis=1)
        return 0

    lax.fori_loop(0, SEQ // ATT_BLOCK, q_block, 0)


def _attn(p_sb):
    head_scratch = pltpu.VMEM((SB_HEADS, SEQ, SB_HEAD_DIM), BF16)
    return pl.pallas_call(
        _attn_kernel,
        grid=(BATCH,),
        in_specs=[pl.BlockSpec((None, SEQ, 3 * MIX_WIDTH), lambda b: (b, 0, 0))],
        out_specs=pl.BlockSpec((None, SEQ, MIX_WIDTH), lambda b: (b, 0, 0)),
        out_shape=jax.ShapeDtypeStruct((BATCH, SEQ, MIX_WIDTH), F32),
        scratch_shapes=[head_scratch, head_scratch, head_scratch,
                        pltpu.VMEM((SB_HEADS, ATT_BLOCK, SB_HEAD_DIM), F32),
                        pltpu.VMEM((SB_HEADS, ATT_BLOCK, 128), F32)],
        compiler_params=_params("parallel"),
        name="attn",
    )(p_sb)


def _merge_kernel(x_ref, ada_ref, gpre_ref, gpost_ref, pc_ref, pch_ref, pp_ref, pph_ref,
                  ys_ref, oa_ref, convw_ref, wg_ref, wco_ref, wglu_ref, wpool_ref, pscale_ref,
                  wpo_ref, wso_ref, wout_ref, o_ref):
    t_tile = pl.program_id(0) % TILES_PER_SEQ
    not_first = (t_tile != 0).astype(F32)
    x = x_ref[...]
    shift = ada_ref[3:4, :]
    scale = ada_ref[4:5, :]
    gate = ada_ref[5:6, :]
    h = (_rms(x, gpre_ref[1:2, :]) * (1.0 + scale) + shift).astype(BF16)

    def gated(branch, y):
        g = _sigmoid(_dot(h, wg_ref[:, branch * D_MODEL:(branch + 1) * D_MODEL]))
        return g * y

    pc = pc_ref[...]
    pch = pch_ref[...]
    u = pc[:, 256:512] * pc[:, 512:768]
    u_halo = pch[:, 256:512] * pch[:, 512:768] * not_first
    u_ext = jnp.concatenate([u_halo, u], axis=0)
    u_1 = pltpu.roll(u_ext, 1, 0)[CONV_HALO:, :]
    u_2 = pltpu.roll(u_ext, 2, 0)[CONV_HALO:, :]
    conv = convw_ref[0:1, :] * u_2 + convw_ref[1:2, :] * u_1 + convw_ref[2:3, :] * u
    merged = gated(0, _dot((pc[:, 0:256] * conv).astype(BF16), wco_ref[...]))

    glu = _dot(ys_ref[...].astype(BF16), wglu_ref[...])
    merged = merged + gated(1, glu[:, :D_MODEL] * _sigmoid(glu[:, D_MODEL:]))

    up = pp_ref[...]
    s1 = jnp.concatenate([pph_ref[...] * not_first, up], axis=0)
    s2 = s1 + pltpu.roll(s1, 1, 0)
    s4 = s2 + pltpu.roll(s2, 2, 0)
    s8 = s4 + pltpu.roll(s4, 4, 0)
    s16 = s8 + pltpu.roll(s8, 8, 0)
    pos1 = (t_tile * ROW_TILE + 1
            + lax.broadcasted_iota(jnp.int32, (ROW_TILE, 1), 0)).astype(F32)
    lane = lax.broadcasted_iota(jnp.int32, (ROW_TILE, 128), 1)
    low = lane < POOL_GROUP

    def window_mean(s, w):
        return s[POOL_HALO:, :] / jnp.minimum(pos1, float(w))

    m_a = jnp.where(low, window_mean(s2, 2)[:, :128], window_mean(s4, 4)[:, :128])
    m_b = jnp.where(low, window_mean(s8, 8)[:, 128:], window_mean(s16, 16)[:, 128:])
    pooled = jnp.concatenate([m_a, m_b], axis=1) - up
    mixed = _dot(pooled.astype(BF16), wpool_ref[...]) * pscale_ref[...]
    merged = merged + gated(2, _dot(mixed.astype(BF16), wpo_ref[...]))

    merged = merged + gated(3, _dot(oa_ref[...].astype(BF16), wso_ref[...]))

    m = _dot(merged.astype(BF16), wout_ref[...])
    o_ref[...] = x + (1.0 + gate) * _rms(m, gpost_ref[1:2, :])


def _merge(x, ada, g_pre, g_post, p_conv, p_pool, y_ssm, o_att, conv_w, w_gate, w_conv_out,
           w_glu, w_pool_bd, pool_scale, w_pool_out, w_sb_out, w_out, layer):
    row = lambda w: pl.BlockSpec((ROW_TILE, w), lambda i: (i, 0))
    halo = lambda n, w: pl.BlockSpec(
        (n, w), lambda i: (jnp.maximum(i * (ROW_TILE // n) - 1, 0), 0))
    per_layer = lambda *shape: _resident((None,) + shape, lambda i: (layer,) + (0,) * len(shape))
    return pl.pallas_call(
        _merge_kernel,
        grid=(TOKENS // ROW_TILE,),
        in_specs=[
            row(D_MODEL),
            pl.BlockSpec((None, None, N_SUB * 3, D_MODEL),
                         lambda i: (layer, i // TILES_PER_SEQ, 0, 0)),
            per_layer(N_SUB, D_MODEL),
            per_layer(N_SUB, D_MODEL),
            row(3 * MIX_WIDTH), halo(CONV_HALO, 3 * MIX_WIDTH),
            row(MIX_WIDTH), halo(POOL_HALO, MIX_WIDTH),
            row(MIX_WIDTH), row(MIX_WIDTH),
            per_layer(CONV_WIDTH, MIX_WIDTH),
            per_layer(D_MODEL, 4 * D_MODEL),
            per_layer(MIX_WIDTH, D_MODEL),
            per_layer(MIX_WIDTH, 2 * D_MODEL),
            per_layer(MIX_WIDTH, MIX_WIDTH),
            per_layer(1, MIX_WIDTH),
            per_layer(MIX_WIDTH, D_MODEL),
            per_layer(MIX_WIDTH, D_MODEL),
            per_layer(D_MODEL, D_MODEL),
        ],
        out_specs=row(D_MODEL),
        out_shape=jax.ShapeDtypeStruct((TOKENS, D_MODEL), F32),
        compiler_params=_params("parallel"),
        name="merge",
    )(x, ada, g_pre, g_post, p_conv, p_conv, p_pool, p_pool, y_ssm, o_att, conv_w, w_gate,
      w_conv_out, w_glu, w_pool_bd, pool_scale, w_pool_out, w_sb_out, w_out)


def _time_major_permutation():
    r = jnp.arange(SSM_ROWS)
    src = (r % BATCH) * SSM_CHUNK + r // BATCH
    return (src[:, None] == r[None, :]).astype(BF16)


def kernel(x, c, w_ada, b_ada, g_pre, g_post, w_ff_in, w_ff_out, w_in, conv_w, w_conv_out,
           lam_re, lam_im, log_dt, ssm_b_re, ssm_b_im, ssm_c_re, ssm_c_im, ssm_d, w_glu, w_pool,
           pool_scale, w_pool_out, w_sb_out, w_out):
    ada = _ada_all(c, w_ada, b_ada)

    w_ff_in_b = w_ff_in.astype(BF16)
    w_ff_out_b = w_ff_out.astype(BF16)
    w_mix_b = w_in[:, :, :MIX_COLS].astype(BF16)
    w_gate_b = w_in[:, :, MIX_COLS:].astype(BF16)
    w_conv_out_b = w_conv_out.astype(BF16)
    w_glu_b = w_glu.astype(BF16)
    w_pool_out_b = w_pool_out.astype(BF16)
    w_sb_out_b = w_sb_out.astype(BF16)
    w_out_b = w_out.astype(BF16)
    eye = jnp.eye(len(POOL_WINDOWS), dtype=F32)
    w_pool_bd = jnp.einsum('lgcd,gk->lgckd', w_pool, eye).reshape(
        DEPTH, MIX_WIDTH, MIX_WIDTH).astype(BF16)
    pool_scale_r = pool_scale.reshape(DEPTH, 1, MIX_WIDTH)
    d_skip = ssm_d.reshape(DEPTH, 1, MIX_WIDTH)
    bmat, cmat, a_bcast = jax.vmap(_ssm_matrices)(
        lam_re, lam_im, log_dt, ssm_b_re, ssm_b_im, ssm_c_re, ssm_c_im)
    perm = _time_major_permutation()
    perm_t = perm.T

    xt = x.reshape(TOKENS, D_MODEL)
    for l in range(DEPTH):
        xt = _ffn(xt, ada, g_pre, g_post, w_ff_in_b, w_ff_out_b, l, 0)
        p_conv, p_ssm, p_pool, p_sb = _inproj(xt, ada, g_pre, w_mix_b, l)
        y_ssm = _ssm(p_ssm.reshape(BATCH, SEQ, MIX_WIDTH), perm, perm_t, bmat, cmat, a_bcast,
                     d_skip, l)
        o_att = _attn(p_sb.reshape(BATCH, SEQ, 3 * MIX_WIDTH))
        xt = _merge(xt, ada, g_pre, g_post, p_conv, p_pool,
                    y_ssm.reshape(TOKENS, MIX_WIDTH), o_att.reshape(TOKENS, MIX_WIDTH),
                    conv_w, w_gate_b, w_conv_out_b, w_glu_b, w_pool_bd, pool_scale_r,
                    w_pool_out_b, w_sb_out_b, w_out_b, l)
        xt = _ffn(xt, ada, g_pre, g_post, w_ff_in_b, w_ff_out_b, l, 1)
    return xt.reshape(BATCH, SEQ, D_MODEL)
```

```python
import functools
import math

import jax
import jax.numpy as jnp
from jax import lax
from jax.experimental import pallas as pl
from jax.experimental.pallas import tpu as pltpu

F32 = jnp.float32
BF16 = jnp.bfloat16

D_MODEL = 1024
BATCH = 8
SEQ = 2048
DEPTH = 4
TOKENS = BATCH * SEQ
MIX_WIDTH = 256
CONV_WIDTH = 3
SSM_GROUP = 16
SSM_GROUPS = 16
SSM_STATE = 64
N_STATE = SSM_GROUPS * SSM_STATE
LAMBDA_RE_MAX = -1e-4
POOL_WINDOWS = (2, 4, 8, 16)
POOL_GROUP = 64
SB_HEAD_DIM = 64
SB_HEADS = 4
D_FF = 2816
FFN_RES_WEIGHT = 0.5
N_SUB = 3
EPS = 1e-6
MIX_COLS = 2048

VMEM_LIMIT_BYTES = 56 * 1024 * 1024

ROW_TILE = 512
TILES_PER_SEQ = SEQ // ROW_TILE
FF_CHUNK = 256
SSM_CHUNK = 128
SSM_ROWS = BATCH * SSM_CHUNK
SSM_BLOCK_STEPS = 16
SSM_BLOCK_ROWS = BATCH * SSM_BLOCK_STEPS
SSM_U_PITCH = SSM_CHUNK + 4
ATT_BLOCK = 256
ATT_Q_BLOCKS = SEQ // ATT_BLOCK
ADA_COL_TILE = 1536
CONV_HALO = 8
POOL_HALO = 16


def _resident(block_shape, index_map):
    return pl.BlockSpec(block_shape, index_map, pipeline_mode=pl.Buffered(1))


def _params(*semantics):
    return pltpu.CompilerParams(dimension_semantics=semantics,
                                vmem_limit_bytes=VMEM_LIMIT_BYTES)


def _rms(x, g):
    ms = jnp.mean(x * x, axis=-1, keepdims=True)
    return x * lax.rsqrt(ms + EPS) * g


def _sigmoid(x):
    return 1.0 / (1.0 + jnp.exp(-x))


def _dot(a, b):
    return jnp.dot(a, b, preferred_element_type=F32)


def _ada_kernel(c_ref, w_ref, b_ref, o_ref):
    c = c_ref[...]
    c_act = (c * _sigmoid(c)).astype(BF16)
    o_ref[...] = _dot(c_act, w_ref[...].astype(BF16)) + b_ref[...]


def _ada_all(c, w_ada, b_ada):
    n_cols = N_SUB * 3 * D_MODEL
    out = pl.pallas_call(
        _ada_kernel,
        grid=(DEPTH, n_cols // ADA_COL_TILE),
        in_specs=[
            pl.BlockSpec((BATCH, D_MODEL), lambda l, j: (0, 0)),
            pl.BlockSpec((None, D_MODEL, ADA_COL_TILE), lambda l, j: (l, 0, j)),
            pl.BlockSpec((None, 1, ADA_COL_TILE), lambda l, j: (l, 0, j)),
        ],
        out_specs=pl.BlockSpec((None, BATCH, ADA_COL_TILE), lambda l, j: (l, 0, j)),
        out_shape=jax.ShapeDtypeStruct((DEPTH, BATCH, n_cols), F32),
        compiler_params=_params("parallel", "parallel"),
        name="ada",
    )(c, w_ada, b_ada.reshape(DEPTH, 1, n_cols))
    return out.reshape(DEPTH, BATCH, N_SUB * 3, D_MODEL)


def _ffn_kernel(sub, x_ref, ada_ref, gpre_ref, gpost_ref, win_ref, wout_ref, o_ref, g_s):
    x = x_ref[...]
    shift = ada_ref[3 * sub:3 * sub + 1, :]
    scale = ada_ref[3 * sub + 1:3 * sub + 2, :]
    gate = ada_ref[3 * sub + 2:3 * sub + 3, :]
    h = (_rms(x, gpre_ref[sub:sub + 1, :]) * (1.0 + scale) + shift).astype(BF16)
    for j in range(D_FF // FF_CHUNK):
        lo = j * FF_CHUNK
        a = _dot(h, win_ref[:, lo:lo + FF_CHUNK])
        b = _dot(h, win_ref[:, D_FF + lo:D_FF + lo + FF_CHUNK])
        g_s[:, lo:lo + FF_CHUNK] = (a * _sigmoid(a) * b).astype(BF16)
    f = _dot(g_s[...], wout_ref[...])
    o_ref[...] = x + FFN_RES_WEIGHT * (1.0 + gate) * _rms(f, gpost_ref[sub:sub + 1, :])


def _ffn(x, ada, g_pre, g_post, w_in, w_out, layer, which):
    sub = 0 if which == 0 else 2
    return pl.pallas_call(
        functools.partial(_ffn_kernel, sub),
        grid=(TOKENS // ROW_TILE,),
        in_specs=[
            pl.BlockSpec((ROW_TILE, D_MODEL), lambda i: (i, 0)),
            pl.BlockSpec((None, None, N_SUB * 3, D_MODEL),
                         lambda i: (layer, i // TILES_PER_SEQ, 0, 0)),
            _resident((None, N_SUB, D_MODEL), lambda i: (layer, 0, 0)),
            _resident((None, N_SUB, D_MODEL), lambda i: (layer, 0, 0)),
            _resident((None, None, D_MODEL, 2 * D_FF), lambda i: (layer, which, 0, 0)),
            _resident((None, None, D_FF, D_MODEL), lambda i: (layer, which, 0, 0)),
        ],
        out_specs=pl.BlockSpec((ROW_TILE, D_MODEL), lambda i: (i, 0)),
        out_shape=jax.ShapeDtypeStruct((TOKENS, D_MODEL), F32),
        scratch_shapes=[pltpu.VMEM((ROW_TILE, D_FF), BF16)],
        compiler_params=_params("parallel"),
        name=f"ffn{which}",
    )(x, ada, g_pre, g_post, w_in, w_out)


def _inproj_kernel(x_ref, ada_ref, gpre_ref, w_ref, oc_ref, os_ref, op_ref, ob_ref):
    x = x_ref[...]
    shift = ada_ref[3:4, :]
    scale = ada_ref[4:5, :]
    h = (_rms(x, gpre_ref[1:2, :]) * (1.0 + scale) + shift).astype(BF16)
    p = _dot(h, w_ref[...])
    oc_ref[...] = p[:, 0:768]
    os_ref[...] = p[:, 768:1024]
    op_ref[...] = p[:, 1024:1280]
    ob_ref[...] = p[:, 1280:2048]


def _inproj(x, ada, g_pre, w_mix, layer):
    widths = (768, 256, 256, 768)
    return pl.pallas_call(
        _inproj_kernel,
        grid=(TOKENS // ROW_TILE,),
        in_specs=[
            pl.BlockSpec((ROW_TILE, D_MODEL), lambda i: (i, 0)),
            pl.BlockSpec((None, None, N_SUB * 3, D_MODEL),
                         lambda i: (layer, i // TILES_PER_SEQ, 0, 0)),
            _resident((None, N_SUB, D_MODEL), lambda i: (layer, 0, 0)),
            _resident((None, D_MODEL, MIX_COLS), lambda i: (layer, 0, 0)),
        ],
        out_specs=[pl.BlockSpec((ROW_TILE, w), lambda i: (i, 0)) for w in widths],
        out_shape=[jax.ShapeDtypeStruct((TOKENS, w), F32) for w in widths],
        compiler_params=_params("parallel"),
        name="inproj",
    )(x, ada, g_pre, w_mix)


def _ssm_kernel(u_ref, bmat_ref, cmat_ref, a_ref, d_ref, o_ref, u_s, bu_s, sb_s, y_s, st_s):
    @pl.when(pl.program_id(0) == 0)
    def _():
        st_s[...] = jnp.zeros_like(st_s)

    slabs = [slice(s * 128, (s + 1) * 128) for s in range(MIX_WIDTH // 128)]
    for b in range(BATCH):
        for s, lanes in enumerate(slabs):
            u_s[s, b * SSM_U_PITCH:b * SSM_U_PITCH + SSM_CHUNK, :] = u_ref[b, :, lanes]

    a_re = a_ref[:, :N_STATE]
    a_im = a_ref[:, N_STATE:]

    def b_proj(k):
        steps = range(k * SSM_BLOCK_STEPS, (k + 1) * SSM_BLOCK_STEPS)
        u_tm = jnp.concatenate(
            [jnp.concatenate([u_s[s, pl.ds(t, BATCH, stride=SSM_U_PITCH), :]
                              for s in range(len(slabs))], axis=1) for t in steps], axis=0)
        bu_s[k * SSM_BLOCK_ROWS:(k + 1) * SSM_BLOCK_ROWS, :] = _dot(u_tm.astype(BF16), bmat_ref[...])

    def scan(k, s_re, s_im):
        for t in range(k * SSM_BLOCK_STEPS, (k + 1) * SSM_BLOCK_STEPS, 2):
            pair_re, pair_im = [], []
            for rows in (slice(t * BATCH, (t + 1) * BATCH), slice((t + 1) * BATCH, (t + 2) * BATCH)):
                s_re, s_im = (a_re * s_re - a_im * s_im + bu_s[rows, :N_STATE],
                              a_re * s_im + a_im * s_re + bu_s[rows, N_STATE:])
                pair_re.append(s_re)
                pair_im.append(s_im)
            rows = slice(t * BATCH, (t + 2) * BATCH)
            sb_s[rows, :N_STATE] = jnp.concatenate(pair_re, axis=0).astype(BF16)
            sb_s[rows, N_STATE:] = jnp.concatenate(pair_im, axis=0).astype(BF16)
        return s_re, s_im

    def c_proj(k):
        rows = slice(k * SSM_BLOCK_ROWS, (k + 1) * SSM_BLOCK_ROWS)
        y = _dot(sb_s[rows, :], cmat_ref[...])
        for s, lanes in enumerate(slabs):
            y_s[s, rows, :] = y[:, lanes]

    n_blocks = SSM_CHUNK // SSM_BLOCK_STEPS
    s_re = st_s[:, :N_STATE]
    s_im = st_s[:, N_STATE:]
    b_proj(0)
    for k in range(n_blocks):
        if k + 1 < n_blocks:
            b_proj(k + 1)
        s_re, s_im = scan(k, s_re, s_im)
        if k >= 1:
            c_proj(k - 1)
    c_proj(n_blocks - 1)
    st_s[:, :N_STATE] = s_re
    st_s[:, N_STATE:] = s_im

    for b in range(BATCH):
        for s, lanes in enumerate(slabs):
            y = y_s[s, pl.ds(b, SSM_CHUNK, stride=BATCH), :]
            o_ref[b, :, lanes] = jax.nn.gelu(y + d_ref[:, lanes] * u_ref[b, :, lanes])


def _ssm(p_ssm, bmat, cmat, a_bcast, d_skip, layer):
    n_slabs = MIX_WIDTH // 128
    return pl.pallas_call(
        _ssm_kernel,
        grid=(SEQ // SSM_CHUNK,),
        in_specs=[
            pl.BlockSpec((BATCH, SSM_CHUNK, MIX_WIDTH), lambda i: (0, i, 0)),
            _resident((None, MIX_WIDTH, 2 * N_STATE), lambda i: (layer, 0, 0)),
            _resident((None, 2 * N_STATE, MIX_WIDTH), lambda i: (layer, 0, 0)),
            _resident((None, BATCH, 2 * N_STATE), lambda i: (layer, 0, 0)),
            _resident((None, 1, MIX_WIDTH), lambda i: (layer, 0, 0)),
        ],
        out_specs=pl.BlockSpec((BATCH, SSM_CHUNK, MIX_WIDTH), lambda i: (0, i, 0)),
        out_shape=jax.ShapeDtypeStruct((BATCH, SEQ, MIX_WIDTH), F32),
        scratch_shapes=[
            pltpu.VMEM((n_slabs, BATCH * SSM_U_PITCH, 128), F32),
            pltpu.VMEM((SSM_ROWS, 2 * N_STATE), F32),
            pltpu.VMEM((SSM_ROWS, 2 * N_STATE), BF16),
            pltpu.VMEM((n_slabs, SSM_ROWS, 128), F32),
            pltpu.VMEM((BATCH, 2 * N_STATE), F32),
        ],
        compiler_params=_params("arbitrary"),
        name="ssm",
    )(p_ssm, bmat, cmat, a_bcast, d_skip)


def _ssm_matrices(lam_re, lam_im, log_dt, b_re, b_im, c_re, c_im):
    lr = jnp.minimum(lam_re, LAMBDA_RE_MAX)
    li = lam_im
    dt = jnp.exp(log_dt)[:, None]
    mag = jnp.exp(lr * dt)
    ab_re = mag * jnp.cos(li * dt)
    ab_im = mag * jnp.sin(li * dt)
    den = lr * lr + li * li
    nr = ab_re - 1.0
    f_re = (nr * lr + ab_im * li) / den
    f_im = (ab_im * lr - nr * li) / den
    bb_re = f_re[..., None] * b_re - f_im[..., None] * b_im
    bb_im = f_re[..., None] * b_im + f_im[..., None] * b_re
    eye = jnp.eye(SSM_GROUPS, dtype=F32)
    to_b = lambda m: jnp.einsum('gph,gk->ghkp', m, eye).reshape(MIX_WIDTH, N_STATE)
    to_c = lambda m: jnp.einsum('ghp,gk->gpkh', m, eye).reshape(N_STATE, MIX_WIDTH)
    bmat = jnp.concatenate([to_b(bb_re), to_b(bb_im)], axis=1).astype(BF16)
    cmat = jnp.concatenate([to_c(c_re), -to_c(c_im)], axis=0).astype(BF16)
    a = jnp.concatenate([ab_re.reshape(N_STATE), ab_im.reshape(N_STATE)])
    return bmat, cmat, jnp.broadcast_to(a, (BATCH, 2 * N_STATE))


def _attn_step(q_rows, k_rows, diagonal, q_s, k_s, v_s, acc_s, r_s, neg_tri, mask):
    heads = range(SB_HEADS)
    zs, hls, ws = [], [], []
    for h in heads:
        z = lax.dot_general(q_s[h, q_rows, :], k_s[h, k_rows, :], (((1,), (1,)), ((), ())),
                            preferred_element_type=F32)
        drop = jnp.maximum(z, 0.0) + jnp.log(1.0 + jnp.exp(-jnp.abs(z)))
        if diagonal:
            drop = jnp.where(mask, drop, 0.0)
        hi = drop.astype(BF16)
        lo = (drop - hi.astype(F32)).astype(BF16)
        zs.append(z)
        hls.append(jnp.concatenate([hi, lo], axis=1))
    for h in heads:
        suffix = _dot(hls[h], neg_tri)
        total = jnp.broadcast_to(suffix[:, 0:1], (ATT_BLOCK, 128))
        log_w = zs[h] + suffix
        if diagonal:
            r_s[h] = total
        else:
            r = r_s[h]
            log_w = log_w + jnp.concatenate([r] * (ATT_BLOCK // 128), axis=1)
            r_s[h] = r + total
        w = jnp.exp(log_w)
        if diagonal:
            w = jnp.where(mask, w, 0.0)
        ws.append(w.astype(BF16))
    for h in heads:
        pv = _dot(ws[h], v_s[h, k_rows, :])
        if diagonal:
            acc_s[h] = pv
        else:
            acc_s[h] += pv


def _attn_kernel(p_ref, o_ref, q_s, k_s, v_s, acc_s, r_s):
    row = lax.broadcasted_iota(jnp.int32, (ATT_BLOCK, ATT_BLOCK), 0)
    col = lax.broadcasted_iota(jnp.int32, (ATT_BLOCK, ATT_BLOCK), 1)
    mask = col < row
    neg_tri = jnp.where(row >= col, -1.0, 0.0).astype(BF16)
    neg_tri = jnp.concatenate([neg_tri, neg_tri], axis=0)
    for h in range(SB_HEADS):
        lanes = slice(h * SB_HEAD_DIM, (h + 1) * SB_HEAD_DIM)
        q_s[h] = (p_ref[:, lanes] * (SB_HEAD_DIM ** -0.5)).astype(BF16)
        k_s[h] = p_ref[:, MIX_WIDTH + lanes.start:MIX_WIDTH + lanes.stop].astype(BF16)
        v_s[h] = p_ref[:, 2 * MIX_WIDTH + lanes.start:2 * MIX_WIDTH + lanes.stop].astype(BF16)

    def q_block(i, _):
        q_rows = pl.ds(pl.multiple_of(i * ATT_BLOCK, ATT_BLOCK), ATT_BLOCK)
        step = functools.partial(_attn_step, q_s=q_s, k_s=k_s, v_s=v_s, acc_s=acc_s, r_s=r_s,
                                 neg_tri=neg_tri, mask=mask)
        step(q_rows, q_rows, True)

        def k_block(j, _):
            step(q_rows, pl.ds(pl.multiple_of((i - j) * ATT_BLOCK, ATT_BLOCK), ATT_BLOCK), False)
            return 0

        lax.fori_loop(1, i + 1, k_block, 0)
        o_ref[q_rows, :] = jnp.concatenate([acc_s[h] for h in range(SB_HEADS)], axis=1)
        return 0

    lax.fori_loop(0, ATT_Q_BLOCKS, q_block, 0)


def _attn(p_sb):
    head_scratch = pltpu.VMEM((SB_HEADS, SEQ, SB_HEAD_DIM), BF16)
    return pl.pallas_call(
        _attn_kernel,
        grid=(BATCH,),
        in_specs=[pl.BlockSpec((None, SEQ, 3 * MIX_WIDTH), lambda b: (b, 0, 0))],
        out_specs=pl.BlockSpec((None, SEQ, MIX_WIDTH), lambda b: (b, 0, 0)),
        out_shape=jax.ShapeDtypeStruct((BATCH, SEQ, MIX_WIDTH), F32),
        scratch_shapes=[head_scratch, head_scratch, head_scratch,
                        pltpu.VMEM((SB_HEADS, ATT_BLOCK, SB_HEAD_DIM), F32),
                        pltpu.VMEM((SB_HEADS, ATT_BLOCK, 128), F32)],
        compiler_params=_params("parallel"),
        name="attn",
    )(p_sb)


def _merge_kernel(x_ref, ada_ref, gpre_ref, gpost_ref, pc_ref, pch_ref, pp_ref, pph_ref,
                  ys_ref, oa_ref, convw_ref, wg01_ref, wg23_ref, wco_ref, wglu_ref, wpool_ref,
                  pscale_ref, wpo_ref, wso_ref, wout_ref, o_ref):
    t_tile = pl.program_id(0) % TILES_PER_SEQ
    not_first = (t_tile != 0).astype(F32)
    x = x_ref[...]
    shift = ada_ref[3:4, :]
    scale = ada_ref[4:5, :]
    gate = ada_ref[5:6, :]
    h = (_rms(x, gpre_ref[1:2, :]) * (1.0 + scale) + shift).astype(BF16)

    def gated(branch, y):
        wg_ref = (wg01_ref, wg23_ref)[branch // 2]
        g = _sigmoid(_dot(h, wg_ref[:, (branch % 2) * D_MODEL:(branch % 2 + 1) * D_MODEL]))
        return g * y

    pc = pc_ref[...]
    pch = pch_ref[...]
    u = pc[:, 256:512] * pc[:, 512:768]
    u_halo = pch[:, 256:512] * pch[:, 512:768] * not_first
    u_ext = jnp.concatenate([u_halo, u], axis=0)
    u_1 = pltpu.roll(u_ext, 1, 0)[CONV_HALO:, :]
    u_2 = pltpu.roll(u_ext, 2, 0)[CONV_HALO:, :]
    conv = convw_ref[0:1, :] * u_2 + convw_ref[1:2, :] * u_1 + convw_ref[2:3, :] * u
    merged = gated(0, _dot((pc[:, 0:256] * conv).astype(BF16), wco_ref[...]))

    glu = _dot(ys_ref[...].astype(BF16), wglu_ref[...])
    merged = merged + gated(1, glu[:, :D_MODEL] * _sigmoid(glu[:, D_MODEL:]))

    up = pp_ref[...]
    s1 = jnp.concatenate([pph_ref[...] * not_first, up], axis=0)
    s2 = s1 + pltpu.roll(s1, 1, 0)
    s4 = s2 + pltpu.roll(s2, 2, 0)
    s8 = s4 + pltpu.roll(s4, 4, 0)
    s16 = s8 + pltpu.roll(s8, 8, 0)
    pos1 = (t_tile * ROW_TILE + 1
            + lax.broadcasted_iota(jnp.int32, (ROW_TILE, 1), 0)).astype(F32)
    lane = lax.broadcasted_iota(jnp.int32, (ROW_TILE, 128), 1)
    low = lane < POOL_GROUP

    def window_mean(s, w):
        return s[POOL_HALO:, :] / jnp.minimum(pos1, float(w))

    m_a = jnp.where(low, window_mean(s2, 2)[:, :128], window_mean(s4, 4)[:, :128])
    m_b = jnp.where(low, window_mean(s8, 8)[:, 128:], window_mean(s16, 16)[:, 128:])
    pooled = jnp.concatenate([m_a, m_b], axis=1) - up
    mixed = _dot(pooled.astype(BF16), wpool_ref[...]) * pscale_ref[...]
    merged = merged + gated(2, _dot(mixed.astype(BF16), wpo_ref[...]))

    merged = merged + gated(3, _dot(oa_ref[...].astype(BF16), wso_ref[...]))

    m = _dot(merged.astype(BF16), wout_ref[...])
    o_ref[...] = x + (1.0 + gate) * _rms(m, gpost_ref[1:2, :])


def _merge(x, ada, g_pre, g_post, p_conv, p_pool, y_ssm, o_att, conv_w, w_in, w_conv_out,
           w_glu, w_pool_bd, pool_scale, w_pool_out, w_sb_out, w_out, layer):
    row = lambda w: pl.BlockSpec((ROW_TILE, w), lambda i: (i, 0))
    halo = lambda n, w: pl.BlockSpec(
        (n, w), lambda i: (jnp.maximum(i * (ROW_TILE // n) - 1, 0), 0))
    per_layer = lambda *shape: _resident((None,) + shape, lambda i: (layer,) + (0,) * len(shape))
    return pl.pallas_call(
        _merge_kernel,
        grid=(TOKENS // ROW_TILE,),
        in_specs=[
            row(D_MODEL),
            pl.BlockSpec((None, None, N_SUB * 3, D_MODEL),
                         lambda i: (layer, i // TILES_PER_SEQ, 0, 0)),
            per_layer(N_SUB, D_MODEL),
            per_layer(N_SUB, D_MODEL),
            row(3 * MIX_WIDTH), halo(CONV_HALO, 3 * MIX_WIDTH),
            row(MIX_WIDTH), halo(POOL_HALO, MIX_WIDTH),
            row(MIX_WIDTH), row(MIX_WIDTH),
            per_layer(CONV_WIDTH, MIX_WIDTH),
            _resident((None, D_MODEL, MIX_COLS), lambda i: (layer, 0, 1)),
            _resident((None, D_MODEL, MIX_COLS), lambda i: (layer, 0, 2)),
            per_layer(MIX_WIDTH, D_MODEL),
            per_layer(MIX_WIDTH, 2 * D_MODEL),
            per_layer(MIX_WIDTH, MIX_WIDTH),
            per_layer(1, MIX_WIDTH),
            per_layer(MIX_WIDTH, D_MODEL),
            per_layer(MIX_WIDTH, D_MODEL),
            per_layer(D_MODEL, D_MODEL),
        ],
        out_specs=row(D_MODEL),
        out_shape=jax.ShapeDtypeStruct((TOKENS, D_MODEL), F32),
        compiler_params=_params("parallel"),
        name="merge",
    )(x, ada, g_pre, g_post, p_conv, p_conv, p_pool, p_pool, y_ssm, o_att, conv_w, w_in, w_in,
      w_conv_out, w_glu, w_pool_bd, pool_scale, w_pool_out, w_sb_out, w_out)


def kernel(x, c, w_ada, b_ada, g_pre, g_post, w_ff_in, w_ff_out, w_in, conv_w, w_conv_out,
           lam_re, lam_im, log_dt, ssm_b_re, ssm_b_im, ssm_c_re, ssm_c_im, ssm_d, w_glu, w_pool,
           pool_scale, w_pool_out, w_sb_out, w_out):
    ada = _ada_all(c, w_ada, b_ada)

    w_ff_in_b = w_ff_in.astype(BF16)
    w_ff_out_b = w_ff_out.astype(BF16)
    w_in_b = w_in.astype(BF16)
    w_conv_out_b = w_conv_out.astype(BF16)
    w_glu_b = w_glu.astype(BF16)
    w_pool_out_b = w_pool_out.astype(BF16)
    w_sb_out_b = w_sb_out.astype(BF16)
    w_out_b = w_out.astype(BF16)
    eye = jnp.eye(len(POOL_WINDOWS), dtype=F32)
    w_pool_bd = jnp.einsum('lgcd,gk->lgckd', w_pool, eye).reshape(
        DEPTH, MIX_WIDTH, MIX_WIDTH).astype(BF16)
    pool_scale_r = pool_scale.reshape(DEPTH, 1, MIX_WIDTH)
    d_skip = ssm_d.reshape(DEPTH, 1, MIX_WIDTH)
    bmat, cmat, a_bcast = jax.vmap(_ssm_matrices)(
        lam_re, lam_im, log_dt, ssm_b_re, ssm_b_im, ssm_c_re, ssm_c_im)

    xt = x.reshape(TOKENS, D_MODEL)
    for l in range(DEPTH):
        xt = _ffn(xt, ada, g_pre, g_post, w_ff_in_b, w_ff_out_b, l, 0)
        p_conv, p_ssm, p_pool, p_sb = _inproj(xt, ada, g_pre, w_in_b, l)
        y_ssm = _ssm(p_ssm.reshape(BATCH, SEQ, MIX_WIDTH), bmat, cmat, a_bcast, d_skip, l)
        o_att = _attn(p_sb.reshape(BATCH, SEQ, 3 * MIX_WIDTH))
        xt = _merge(xt, ada, g_pre, g_post, p_conv, p_pool,
                    y_ssm.reshape(TOKENS, MIX_WIDTH), o_att.reshape(TOKENS, MIX_WIDTH),
                    conv_w, w_in_b, w_conv_out_b, w_glu_b, w_pool_bd, pool_scale_r,
                    w_pool_out_b, w_sb_out_b, w_out_b, l)
        xt = _ffn(xt, ada, g_pre, g_post, w_ff_in_b, w_ff_out_b, l, 1)
    return xt.reshape(BATCH, SEQ, D_MODEL)
```

```python
import functools
import math

import jax
import jax.numpy as jnp
from jax import lax
from jax.experimental import pallas as pl
from jax.experimental.pallas import tpu as pltpu

F32 = jnp.float32
BF16 = jnp.bfloat16

D_MODEL = 1024
BATCH = 8
SEQ = 2048
DEPTH = 4
TOKENS = BATCH * SEQ
MIX_WIDTH = 256
CONV_WIDTH = 3
SSM_GROUP = 16
SSM_GROUPS = 16
SSM_STATE = 64
N_STATE = SSM_GROUPS * SSM_STATE
LAMBDA_RE_MAX = -1e-4
POOL_WINDOWS = (2, 4, 8, 16)
POOL_GROUP = 64
SB_HEAD_DIM = 64
SB_HEADS = 4
D_FF = 2816
FFN_RES_WEIGHT = 0.5
N_SUB = 3
EPS = 1e-6
MIX_COLS = 2048

VMEM_LIMIT_BYTES = 56 * 1024 * 1024

ROW_TILE = 512
TILES_PER_SEQ = SEQ // ROW_TILE
FFN_ROW_TILE = 1024
FF_CHUNK = 256
SSM_CHUNK = 128
SSM_ROWS = BATCH * SSM_CHUNK
SSM_BLOCK_STEPS = 16
SSM_BLOCK_ROWS = BATCH * SSM_BLOCK_STEPS
SSM_U_PITCH = SSM_CHUNK + 4
ATT_BLOCK = 256
ATT_Q_BLOCKS = SEQ // ATT_BLOCK
ADA_COL_TILE = 1536
CONV_HALO = 8
POOL_HALO = 16


def _resident(block_shape, index_map):
    return pl.BlockSpec(block_shape, index_map, pipeline_mode=pl.Buffered(1))


def _params(*semantics):
    return pltpu.CompilerParams(dimension_semantics=semantics,
                                vmem_limit_bytes=VMEM_LIMIT_BYTES)


def _rms(x, g):
    ms = jnp.mean(x * x, axis=-1, keepdims=True)
    return x * lax.rsqrt(ms + EPS) * g


def _sigmoid(x):
    return 1.0 / (1.0 + jnp.exp(-x))


def _dot(a, b):
    return jnp.dot(a, b, preferred_element_type=F32)


def _ada_kernel(c_ref, w_ref, b_ref, o_ref):
    c = c_ref[...]
    c_act = (c * _sigmoid(c)).astype(BF16)
    o_ref[...] = _dot(c_act, w_ref[...].astype(BF16)) + b_ref[...]


def _ada_all(c, w_ada, b_ada):
    n_cols = N_SUB * 3 * D_MODEL
    out = pl.pallas_call(
        _ada_kernel,
        grid=(DEPTH, n_cols // ADA_COL_TILE),
        in_specs=[
            pl.BlockSpec((BATCH, D_MODEL), lambda l, j: (0, 0)),
            pl.BlockSpec((None, D_MODEL, ADA_COL_TILE), lambda l, j: (l, 0, j)),
            pl.BlockSpec((None, 1, ADA_COL_TILE), lambda l, j: (l, 0, j)),
        ],
        out_specs=pl.BlockSpec((None, BATCH, ADA_COL_TILE), lambda l, j: (l, 0, j)),
        out_shape=jax.ShapeDtypeStruct((DEPTH, BATCH, n_cols), F32),
        compiler_params=_params("parallel", "parallel"),
        name="ada",
    )(c, w_ada, b_ada.reshape(DEPTH, 1, n_cols))
    return out.reshape(DEPTH, BATCH, N_SUB * 3, D_MODEL)


def _ffn_kernel(sub, x_ref, ada_ref, gpre_ref, gpost_ref, win_ref, wout_ref, o_ref, g_s):
    shift = ada_ref[3 * sub:3 * sub + 1, :]
    scale = ada_ref[3 * sub + 1:3 * sub + 2, :]
    gate = ada_ref[3 * sub + 2:3 * sub + 3, :]
    halves = [slice(r, r + FFN_ROW_TILE // 2) for r in (0, FFN_ROW_TILE // 2)]
    hs = [(_rms(x_ref[rows, :], gpre_ref[sub:sub + 1, :]) * (1.0 + scale) + shift).astype(BF16)
          for rows in halves]
    for rows, h in zip(halves, hs):
        for j in range(D_FF // FF_CHUNK):
            lo = j * FF_CHUNK
            a = _dot(h, win_ref[:, lo:lo + FF_CHUNK])
            b = _dot(h, win_ref[:, D_FF + lo:D_FF + lo + FF_CHUNK])
            g_s[rows, lo:lo + FF_CHUNK] = (a * _sigmoid(a) * b).astype(BF16)
        f = _dot(g_s[rows, :], wout_ref[...])
        o_ref[rows, :] = (x_ref[rows, :]
                          + FFN_RES_WEIGHT * (1.0 + gate) * _rms(f, gpost_ref[sub:sub + 1, :]))


def _ffn(x, ada, g_pre, g_post, w_in, w_out, layer, which):
    sub = 0 if which == 0 else 2
    return pl.pallas_call(
        functools.partial(_ffn_kernel, sub),
        grid=(TOKENS // FFN_ROW_TILE,),
        in_specs=[
            pl.BlockSpec((FFN_ROW_TILE, D_MODEL), lambda i: (i, 0)),
            pl.BlockSpec((None, None, N_SUB * 3, D_MODEL),
                         lambda i: (layer, i // (SEQ // FFN_ROW_TILE), 0, 0)),
            _resident((None, N_SUB, D_MODEL), lambda i: (layer, 0, 0)),
            _resident((None, N_SUB, D_MODEL), lambda i: (layer, 0, 0)),
            _resident((None, None, D_MODEL, 2 * D_FF), lambda i: (layer, which, 0, 0)),
            _resident((None, None, D_FF, D_MODEL), lambda i: (layer, which, 0, 0)),
        ],
        out_specs=pl.BlockSpec((FFN_ROW_TILE, D_MODEL), lambda i: (i, 0)),
        out_shape=jax.ShapeDtypeStruct((TOKENS, D_MODEL), F32),
        scratch_shapes=[pltpu.VMEM((FFN_ROW_TILE, D_FF), BF16)],
        compiler_params=_params("parallel"),
        name=f"ffn{which}",
    )(x, ada, g_pre, g_post, w_in, w_out)


def _inproj_kernel(x_ref, ada_ref, gpre_ref, w_ref, oc_ref, os_ref, op_ref, ob_ref):
    x = x_ref[...]
    shift = ada_ref[3:4, :]
    scale = ada_ref[4:5, :]
    h = (_rms(x, gpre_ref[1:2, :]) * (1.0 + scale) + shift).astype(BF16)
    p = _dot(h, w_ref[...])
    oc_ref[...] = p[:, 0:768]
    os_ref[...] = p[:, 768:1024]
    op_ref[...] = p[:, 1024:1280]
    ob_ref[...] = p[:, 1280:2048]


def _inproj(x, ada, g_pre, w_mix, layer):
    widths = (768, 256, 256, 768)
    return pl.pallas_call(
        _inproj_kernel,
        grid=(TOKENS // ROW_TILE,),
        in_specs=[
            pl.BlockSpec((ROW_TILE, D_MODEL), lambda i: (i, 0)),
            pl.BlockSpec((None, None, N_SUB * 3, D_MODEL),
                         lambda i: (layer, i // TILES_PER_SEQ, 0, 0)),
            _resident((None, N_SUB, D_MODEL), lambda i: (layer, 0, 0)),
            _resident((None, D_MODEL, MIX_COLS), lambda i: (layer, 0, 0)),
        ],
        out_specs=[pl.BlockSpec((ROW_TILE, w), lambda i: (i, 0)) for w in widths],
        out_shape=[jax.ShapeDtypeStruct((TOKENS, w), F32) for w in widths],
        compiler_params=_params("parallel"),
        name="inproj",
    )(x, ada, g_pre, w_mix)


def _ssm_kernel(u_ref, bmat_ref, cmat_ref, a_ref, d_ref, o_ref, u_s, bu_s, sb_s, y_s, st_s):
    @pl.when(pl.program_id(0) == 0)
    def _():
        st_s[...] = jnp.zeros_like(st_s)

    slabs = [slice(s * 128, (s + 1) * 128) for s in range(MIX_WIDTH // 128)]
    for b in range(BATCH):
        for s, lanes in enumerate(slabs):
            u_s[s, b * SSM_U_PITCH:b * SSM_U_PITCH + SSM_CHUNK, :] = u_ref[b, :, lanes]

    a_re = a_ref[:, :N_STATE]
    a_im = a_ref[:, N_STATE:]

    def b_proj(k):
        steps = range(k * SSM_BLOCK_STEPS, (k + 1) * SSM_BLOCK_STEPS)
        u_tm = jnp.concatenate(
            [jnp.concatenate([u_s[s, pl.ds(t, BATCH, stride=SSM_U_PITCH), :]
                              for s in range(len(slabs))], axis=1) for t in steps], axis=0)
        bu_s[k * SSM_BLOCK_ROWS:(k + 1) * SSM_BLOCK_ROWS, :] = _dot(u_tm.astype(BF16), bmat_ref[...])

    def scan(k, s_re, s_im):
        for t in range(k * SSM_BLOCK_STEPS, (k + 1) * SSM_BLOCK_STEPS, 2):
            pair_re, pair_im = [], []
            for rows in (slice(t * BATCH, (t + 1) * BATCH), slice((t + 1) * BATCH, (t + 2) * BATCH)):
                s_re, s_im = (a_re * s_re - a_im * s_im + bu_s[rows, :N_STATE],
                              a_re * s_im + a_im * s_re + bu_s[rows, N_STATE:])
                pair_re.append(s_re)
                pair_im.append(s_im)
            rows = slice(t * BATCH, (t + 2) * BATCH)
            sb_s[rows, :N_STATE] = jnp.concatenate(pair_re, axis=0).astype(BF16)
            sb_s[rows, N_STATE:] = jnp.concatenate(pair_im, axis=0).astype(BF16)
        return s_re, s_im

    def c_proj(k):
        rows = slice(k * SSM_BLOCK_ROWS, (k + 1) * SSM_BLOCK_ROWS)
        y = _dot(sb_s[rows, :], cmat_ref[...])
        for s, lanes in enumerate(slabs):
            y_s[s, rows, :] = y[:, lanes]

    n_blocks = SSM_CHUNK // SSM_BLOCK_STEPS
    s_re = st_s[:, :N_STATE]
    s_im = st_s[:, N_STATE:]
    b_proj(0)
    for k in range(n_blocks):
        if k + 1 < n_blocks:
            b_proj(k + 1)
        s_re, s_im = scan(k, s_re, s_im)
        if k >= 1:
            c_proj(k - 1)
    c_proj(n_blocks - 1)
    st_s[:, :N_STATE] = s_re
    st_s[:, N_STATE:] = s_im

    for b in range(BATCH):
        for s, lanes in enumerate(slabs):
            y = y_s[s, pl.ds(b, SSM_CHUNK, stride=BATCH), :]
            o_ref[b, :, lanes] = jax.nn.gelu(y + d_ref[:, lanes] * u_ref[b, :, lanes])


def _ssm(p_ssm, bmat, cmat, a_bcast, d_skip, layer):
    n_slabs = MIX_WIDTH // 128
    return pl.pallas_call(
        _ssm_kernel,
        grid=(SEQ // SSM_CHUNK,),
        in_specs=[
            pl.BlockSpec((BATCH, SSM_CHUNK, MIX_WIDTH), lambda i: (0, i, 0)),
            _resident((None, MIX_WIDTH, 2 * N_STATE), lambda i: (layer, 0, 0)),
            _resident((None, 2 * N_STATE, MIX_WIDTH), lambda i: (layer, 0, 0)),
            _resident((None, BATCH, 2 * N_STATE), lambda i: (layer, 0, 0)),
            _resident((None, 1, MIX_WIDTH), lambda i: (layer, 0, 0)),
        ],
        out_specs=pl.BlockSpec((BATCH, SSM_CHUNK, MIX_WIDTH), lambda i: (0, i, 0)),
        out_shape=jax.ShapeDtypeStruct((BATCH, SEQ, MIX_WIDTH), F32),
        scratch_shapes=[
            pltpu.VMEM((n_slabs, BATCH * SSM_U_PITCH, 128), F32),
            pltpu.VMEM((SSM_ROWS, 2 * N_STATE), F32),
            pltpu.VMEM((SSM_ROWS, 2 * N_STATE), BF16),
            pltpu.VMEM((n_slabs, SSM_ROWS, 128), F32),
            pltpu.VMEM((BATCH, 2 * N_STATE), F32),
        ],
        compiler_params=_params("arbitrary"),
        name="ssm",
    )(p_ssm, bmat, cmat, a_bcast, d_skip)


def _ssm_matrices(lam_re, lam_im, log_dt, b_re, b_im, c_re, c_im):
    lr = jnp.minimum(lam_re, LAMBDA_RE_MAX)
    li = lam_im
    dt = jnp.exp(log_dt)[:, None]
    mag = jnp.exp(lr * dt)
    ab_re = mag * jnp.cos(li * dt)
    ab_im = mag * jnp.sin(li * dt)
    den = lr * lr + li * li
    nr = ab_re - 1.0
    f_re = (nr * lr + ab_im * li) / den
    f_im = (ab_im * lr - nr * li) / den
    bb_re = f_re[..., None] * b_re - f_im[..., None] * b_im
    bb_im = f_re[..., None] * b_im + f_im[..., None] * b_re
    eye = jnp.eye(SSM_GROUPS, dtype=F32)
    to_b = lambda m: jnp.einsum('gph,gk->ghkp', m, eye).reshape(MIX_WIDTH, N_STATE)
    to_c = lambda m: jnp.einsum('ghp,gk->gpkh', m, eye).reshape(N_STATE, MIX_WIDTH)
    bmat = jnp.concatenate([to_b(bb_re), to_b(bb_im)], axis=1).astype(BF16)
    cmat = jnp.concatenate([to_c(c_re), -to_c(c_im)], axis=0).astype(BF16)
    a = jnp.concatenate([ab_re.reshape(N_STATE), ab_im.reshape(N_STATE)])
    return bmat, cmat, jnp.broadcast_to(a, (BATCH, 2 * N_STATE))


def _attn_step(q_rows, k_rows, diagonal, q_s, k_s, v_s, acc_s, r_s, neg_tri, mask):
    heads = range(SB_HEADS)
    zs, drops, ws = {}, {}, {}
    for c, rows in enumerate(k_rows):
        for h in heads:
            z = lax.dot_general(q_s[h, q_rows, :], k_s[h, rows, :], (((1,), (1,)), ((), ())),
                                preferred_element_type=F32)
            drop = jnp.maximum(z, 0.0) + jnp.log(1.0 + jnp.exp(-jnp.abs(z)))
            if diagonal:
                drop = jnp.where(mask, drop, 0.0)
            zs[c, h] = z
            drops[c, h] = drop.astype(BF16)
    for h in heads:
        r = None if diagonal else r_s[h]
        for c, rows in enumerate(k_rows):
            suffix = _dot(drops[c, h], neg_tri)
            log_w = zs[c, h] + suffix
            if r is not None:
                log_w = log_w + jnp.concatenate([r] * (ATT_BLOCK // 128), axis=1)
            w = jnp.exp(log_w)
            if diagonal:
                w = jnp.where(mask, w, 0.0)
            ws[c, h] = w.astype(BF16)
            total = jnp.broadcast_to(suffix[:, 0:1], (ATT_BLOCK, 128))
            r = total if r is None else r + total
        r_s[h] = r
    for h in heads:
        pv = sum(_dot(ws[c, h], v_s[h, rows, :]) for c, rows in enumerate(k_rows))
        if diagonal:
            acc_s[h] = pv
        else:
            acc_s[h] += pv


def _attn_kernel(p_ref, o_ref, q_s, k_s, v_s, acc_s, r_s):
    row = lax.broadcasted_iota(jnp.int32, (ATT_BLOCK, ATT_BLOCK), 0)
    col = lax.broadcasted_iota(jnp.int32, (ATT_BLOCK, ATT_BLOCK), 1)
    mask = col < row
    neg_tri = jnp.where(row >= col, -1.0, 0.0).astype(BF16)
    for h in range(SB_HEADS):
        lanes = slice(h * SB_HEAD_DIM, (h + 1) * SB_HEAD_DIM)
        q_s[h] = (p_ref[:, lanes] * (SB_HEAD_DIM ** -0.5)).astype(BF16)
        k_s[h] = p_ref[:, MIX_WIDTH + lanes.start:MIX_WIDTH + lanes.stop].astype(BF16)
        v_s[h] = p_ref[:, 2 * MIX_WIDTH + lanes.start:2 * MIX_WIDTH + lanes.stop].astype(BF16)

    def q_block(i, _):
        q_rows = pl.ds(pl.multiple_of(i * ATT_BLOCK, ATT_BLOCK), ATT_BLOCK)
        step = functools.partial(_attn_step, q_s=q_s, k_s=k_s, v_s=v_s, acc_s=acc_s, r_s=r_s,
                                 neg_tri=neg_tri, mask=mask)
        rows = lambda block: pl.ds(pl.multiple_of(block * ATT_BLOCK, ATT_BLOCK), ATT_BLOCK)
        step(q_rows, [q_rows], True)

        def k_pair(p, _):
            step(q_rows, [rows(i - 1 - 2 * p), rows(i - 2 - 2 * p)], False)
            return 0

        lax.fori_loop(0, i // 2, k_pair, 0)

        @pl.when(i % 2 == 1)
        def _():
            step(q_rows, [rows(0)], False)

        o_ref[q_rows, :] = jnp.concatenate([acc_s[h] for h in range(SB_HEADS)], axis=1)
        return 0

    lax.fori_loop(0, ATT_Q_BLOCKS, q_block, 0)


def _attn(p_sb):
    head_scratch = pltpu.VMEM((SB_HEADS, SEQ, SB_HEAD_DIM), BF16)
    return pl.pallas_call(
        _attn_kernel,
        grid=(BATCH,),
        in_specs=[pl.BlockSpec((None, SEQ, 3 * MIX_WIDTH), lambda b: (b, 0, 0))],
        out_specs=pl.BlockSpec((None, SEQ, MIX_WIDTH), lambda b: (b, 0, 0)),
        out_shape=jax.ShapeDtypeStruct((BATCH, SEQ, MIX_WIDTH), F32),
        scratch_shapes=[head_scratch, head_scratch, head_scratch,
                        pltpu.VMEM((SB_HEADS, ATT_BLOCK, SB_HEAD_DIM), F32),
                        pltpu.VMEM((SB_HEADS, ATT_BLOCK, 128), F32)],
        compiler_params=_params("parallel"),
        name="attn",
    )(p_sb)


def _merge_kernel(x_ref, ada_ref, gpre_ref, gpost_ref, pc_ref, pch_ref, pp_ref, pph_ref,
                  ys_ref, oa_ref, convw_ref, wg01_ref, wg23_ref, wco_ref, wglu_ref, wpool_ref,
                  pscale_ref, wpo_ref, wso_ref, wout_ref, o_ref):
    t_tile = pl.program_id(0) % TILES_PER_SEQ
    not_first = (t_tile != 0).astype(F32)
    x = x_ref[...]
    shift = ada_ref[3:4, :]
    scale = ada_ref[4:5, :]
    gate = ada_ref[5:6, :]
    h = (_rms(x, gpre_ref[1:2, :]) * (1.0 + scale) + shift).astype(BF16)

    def gated(branch, y):
        wg_ref = (wg01_ref, wg23_ref)[branch // 2]
        g = _sigmoid(_dot(h, wg_ref[:, (branch % 2) * D_MODEL:(branch % 2 + 1) * D_MODEL]))
        return g * y

    pc = pc_ref[...]
    pch = pch_ref[...]
    u = pc[:, 256:512] * pc[:, 512:768]
    u_halo = pch[:, 256:512] * pch[:, 512:768] * not_first
    u_ext = jnp.concatenate([u_halo, u], axis=0)
    u_1 = pltpu.roll(u_ext, 1, 0)[CONV_HALO:, :]
    u_2 = pltpu.roll(u_ext, 2, 0)[CONV_HALO:, :]
    conv = convw_ref[0:1, :] * u_2 + convw_ref[1:2, :] * u_1 + convw_ref[2:3, :] * u
    merged = gated(0, _dot((pc[:, 0:256] * conv).astype(BF16), wco_ref[...]))

    glu = _dot(ys_ref[...].astype(BF16), wglu_ref[...])
    merged = merged + gated(1, glu[:, :D_MODEL] * _sigmoid(glu[:, D_MODEL:]))

    up = pp_ref[...]
    s1 = jnp.concatenate([pph_ref[...] * not_first, up], axis=0)
    s2 = s1 + pltpu.roll(s1, 1, 0)
    s4 = s2 + pltpu.roll(s2, 2, 0)
    s8 = s4 + pltpu.roll(s4, 4, 0)
    s16 = s8 + pltpu.roll(s8, 8, 0)
    pos1 = (t_tile * ROW_TILE + 1
            + lax.broadcasted_iota(jnp.int32, (ROW_TILE, 1), 0)).astype(F32)
    lane = lax.broadcasted_iota(jnp.int32, (ROW_TILE, 128), 1)
    low = lane < POOL_GROUP

    def window_mean(s, w):
        return s[POOL_HALO:, :] / jnp.minimum(pos1, float(w))

    m_a = jnp.where(low, window_mean(s2, 2)[:, :128], window_mean(s4, 4)[:, :128])
    m_b = jnp.where(low, window_mean(s8, 8)[:, 128:], window_mean(s16, 16)[:, 128:])
    pooled = jnp.concatenate([m_a, m_b], axis=1) - up
    mixed = _dot(pooled.astype(BF16), wpool_ref[...]) * pscale_ref[...]
    merged = merged + gated(2, _dot(mixed.astype(BF16), wpo_ref[...]))

    merged = merged + gated(3, _dot(oa_ref[...].astype(BF16), wso_ref[...]))

    m = _dot(merged.astype(BF16), wout_ref[...])
    o_ref[...] = x + (1.0 + gate) * _rms(m, gpost_ref[1:2, :])


def _merge(x, ada, g_pre, g_post, p_conv, p_pool, y_ssm, o_att, conv_w, w_in, w_conv_out,
           w_glu, w_pool_bd, pool_scale, w_pool_out, w_sb_out, w_out, layer):
    row = lambda w: pl.BlockSpec((ROW_TILE, w), lambda i: (i, 0))
    halo = lambda n, w: pl.BlockSpec(
        (n, w), lambda i: (jnp.maximum(i * (ROW_TILE // n) - 1, 0), 0))
    per_layer = lambda *shape: _resident((None,) + shape, lambda i: (layer,) + (0,) * len(shape))
    return pl.pallas_call(
        _merge_kernel,
        grid=(TOKENS // ROW_TILE,),
        in_specs=[
            row(D_MODEL),
            pl.BlockSpec((None, None, N_SUB * 3, D_MODEL),
                         lambda i: (layer, i // TILES_PER_SEQ, 0, 0)),
            per_layer(N_SUB, D_MODEL),
            per_layer(N_SUB, D_MODEL),
            row(3 * MIX_WIDTH), halo(CONV_HALO, 3 * MIX_WIDTH),
            row(MIX_WIDTH), halo(POOL_HALO, MIX_WIDTH),
            row(MIX_WIDTH), row(MIX_WIDTH),
            per_layer(CONV_WIDTH, MIX_WIDTH),
            _resident((None, D_MODEL, MIX_COLS), lambda i: (layer, 0, 1)),
            _resident((None, D_MODEL, MIX_COLS), lambda i: (layer, 0, 2)),
            per_layer(MIX_WIDTH, D_MODEL),
            per_layer(MIX_WIDTH, 2 * D_MODEL),
            per_layer(MIX_WIDTH, MIX_WIDTH),
            per_layer(1, MIX_WIDTH),
            per_layer(MIX_WIDTH, D_MODEL),
            per_layer(MIX_WIDTH, D_MODEL),
            per_layer(D_MODEL, D_MODEL),
        ],
        out_specs=row(D_MODEL),
        out_shape=jax.ShapeDtypeStruct((TOKENS, D_MODEL), F32),
        compiler_params=_params("parallel"),
        name="merge",
    )(x, ada, g_pre, g_post, p_conv, p_conv, p_pool, p_pool, y_ssm, o_att, conv_w, w_in, w_in,
      w_conv_out, w_glu, w_pool_bd, pool_scale, w_pool_out, w_sb_out, w_out)


def kernel(x, c, w_ada, b_ada, g_pre, g_post, w_ff_in, w_ff_out, w_in, conv_w, w_conv_out,
           lam_re, lam_im, log_dt, ssm_b_re, ssm_b_im, ssm_c_re, ssm_c_im, ssm_d, w_glu, w_pool,
           pool_scale, w_pool_out, w_sb_out, w_out):
    ada = _ada_all(c, w_ada, b_ada)

    w_ff_in_b = w_ff_in.astype(BF16)
    w_ff_out_b = w_ff_out.astype(BF16)
    w_in_b = w_in.astype(BF16)
    w_conv_out_b = w_conv_out.astype(BF16)
    w_glu_b = w_glu.astype(BF16)
    w_pool_out_b = w_pool_out.astype(BF16)
    w_sb_out_b = w_sb_out.astype(BF16)
    w_out_b = w_out.astype(BF16)
    eye = jnp.eye(len(POOL_WINDOWS), dtype=F32)
    w_pool_bd = jnp.einsum('lgcd,gk->lgckd', w_pool, eye).reshape(
        DEPTH, MIX_WIDTH, MIX_WIDTH).astype(BF16)
    pool_scale_r = pool_scale.reshape(DEPTH, 1, MIX_WIDTH)
    d_skip = ssm_d.reshape(DEPTH, 1, MIX_WIDTH)
    bmat, cmat, a_bcast = jax.vmap(_ssm_matrices)(
        lam_re, lam_im, log_dt, ssm_b_re, ssm_b_im, ssm_c_re, ssm_c_im)

    xt = x.reshape(TOKENS, D_MODEL)
    for l in range(DEPTH):
        xt = _ffn(xt, ada, g_pre, g_post, w_ff_in_b, w_ff_out_b, l, 0)
        p_conv, p_ssm, p_pool, p_sb = _inproj(xt, ada, g_pre, w_in_b, l)
        y_ssm = _ssm(p_ssm.reshape(BATCH, SEQ, MIX_WIDTH), bmat, cmat, a_bcast, d_skip, l)
        o_att = _attn(p_sb.reshape(BATCH, SEQ, 3 * MIX_WIDTH))
        xt = _merge(xt, ada, g_pre, g_post, p_conv, p_pool,
                    y_ssm.reshape(TOKENS, MIX_WIDTH), o_att.reshape(TOKENS, MIX_WIDTH),
                    conv_w, w_in_b, w_conv_out_b, w_glu_b, w_pool_bd, pool_scale_r,
                    w_pool_out_b, w_sb_out_b, w_out_b, l)
        xt = _ffn(xt, ada, g_pre, g_post, w_ff_in_b, w_ff_out_b, l, 1)
    return xt.reshape(BATCH, SEQ, D_MODEL)
```
